```python
import math
import jax, jax.numpy as jnp
from jax import lax
import numpy as np

D_MODEL = 1024
BATCH = 4
SEQ = 4096
DEPTH = 2
DEC_BATCH = 32
DEC_SEQ = 4
PAST_LEN = 8192
PAGE_SIZE = 128

MIX_DIM = D_MODEL
DN_K_DIM = 128
DN_V_DIM = 128
DN_HEADS = (MIX_DIM // 2) // DN_V_DIM
DN_DIM = DN_HEADS * DN_K_DIM
SB_HEAD_DIM = 64
SB_HEADS = (MIX_DIM // 2) // SB_HEAD_DIM
SB_DIM = SB_HEADS * SB_HEAD_DIM
SB_BIAS_INIT = -6.0
CONV_W = 4
CHUNK = 64
Q_BLOCK = 128
D_FF = 2816
QKV_COLS = 3 * DN_DIM
IN_COLS = QKV_COLS + DN_HEADS * DN_V_DIM + 2 * DN_HEADS + 3 * SB_DIM
EPS = 1e-6

kernel_name = "hymba_deltanet_stickbreak_macaron_step"


def rms_norm(x, gain):
    xf = x.astype(jnp.float32)
    y = xf * lax.rsqrt(jnp.mean(xf * xf, axis=-1, keepdims=True) + EPS)
    return (y * gain.astype(jnp.float32)).astype(x.dtype)


def l2_normalize(x):
    xf = x.astype(jnp.float32)
    return xf * lax.rsqrt(jnp.sum(xf * xf, axis=-1, keepdims=True) + EPS)


def swiglu_ffn(x, w_gate_up, w_down):
    gate, up = jnp.split(x @ w_gate_up, 2, axis=-1)
    return (jax.nn.silu(gate) * up) @ w_down


def causal_short_conv(x, hist, w):
    L = x.shape[1]
    xp = jnp.concatenate([hist.astype(x.dtype), x], axis=1)
    y = xp[:, 0:L] * w[0]
    for j in range(1, CONV_W):
        y = y + xp[:, j:j + L] * w[j]
    return jax.nn.silu(y), xp[:, xp.shape[1] - (CONV_W - 1):]


def gated_delta_rule(q, k, v, g, beta, s0):
    B, L, H, dk = q.shape
    dv = v.shape[-1]
    C = min(CHUNK, L)
    n = -(-L // C)
    pad = n * C - L

    def chunks(t):
        t = jnp.pad(t.astype(jnp.float32), [(0, 0), (0, pad)] + [(0, 0)] * (t.ndim - 2))
        t = t.reshape((B, n, C) + t.shape[2:])
        return jnp.moveaxis(jnp.moveaxis(t, 1, 0), 3, 2)

    qc, kc, vc = chunks(q), chunks(k), chunks(v)
    gc, bc = chunks(g), chunks(beta)
    G = jnp.cumsum(gc, axis=-1)
    idx = jnp.arange(C)
    incl = idx[:, None] >= idx[None, :]
    strict = idx[:, None] > idx[None, :]
    decay = jnp.exp(jnp.where(incl, G[..., :, None] - G[..., None, :], -jnp.inf))
    kb = kc * bc[..., None]
    m = jnp.where(strict, jnp.einsum('nbhtk,nbhsk->nbhts', kb, kc) * decay, 0.0)
    rhs = jnp.concatenate([vc * bc[..., None], kb * jnp.exp(G)[..., None]], axis=-1)
    sol = lax.linalg.triangular_solve(m + jnp.eye(C, dtype=m.dtype), rhs,
                                      left_side=True, lower=True, unit_diagonal=True)
    u_base, w_state = sol[..., :dv], sol[..., dv:]
    qk = jnp.einsum('nbhtk,nbhsk->nbhts', qc, kc) * decay

    def step(s, inp):
        q_i, k_i, u_i, w_i, g_i, qk_i = inp
        u = u_i - jnp.einsum('bhck,bhkv->bhcv', w_i, s)
        o = (jnp.einsum('bhck,bhkv->bhcv', q_i * jnp.exp(g_i)[..., None], s)
             + jnp.einsum('bhts,bhsv->bhtv', qk_i, u))
        g_last = g_i[..., -1]
        s = (s * jnp.exp(g_last)[..., None, None]
             + jnp.einsum('bhck,bhcv->bhkv', k_i * jnp.exp(g_last[..., None] - g_i)[..., None], u))
        return s, o

    s, o = lax.scan(step, s0.astype(jnp.float32), (qc, kc, u_base, w_state, G, qk))
    o = jnp.swapaxes(jnp.moveaxis(o, 0, 1), 2, 3).reshape(B, n * C, H, dv)[:, :L]
    return o, s


def stick_breaking_block(q_blk, q_pos, k, v, k_pos, bias):
    z = (jnp.einsum('bqhd,bkhd->bhqk', q_blk.astype(jnp.float32), k.astype(jnp.float32)) * (SB_HEAD_DIM ** -0.5)
         + bias.astype(jnp.float32)[None, :, None, None])
    mask = k_pos[None, :] < q_pos[:, None]
    log_not = jnp.where(mask, jax.nn.log_sigmoid(-z), 0.0)
    between = lax.cumsum(log_not, axis=3, reverse=True) - log_not
    a = jnp.where(mask, jnp.exp(jax.nn.log_sigmoid(z) + between), 0.0)
    return jnp.einsum('bhqk,bkhd->bqhd', a, v.astype(jnp.float32))


def stick_breaking_attention(q, k, v, q_start, bias):
    B, L, H, d = q.shape
    qb = min(Q_BLOCK, L)
    nb = -(-L // qb)
    pad = nb * qb - L
    qp = jnp.moveaxis(jnp.pad(q, ((0, 0), (0, pad), (0, 0), (0, 0))).reshape(B, nb, qb, H, d), 1, 0)
    pos = (q_start + jnp.arange(nb * qb)).reshape(nb, qb)
    k_pos = jnp.arange(k.shape[1])
    o = lax.map(lambda a: stick_breaking_block(a[0], a[1], k, v, k_pos, bias), (qp, pos))
    return jnp.moveaxis(o, 0, 1).reshape(B, nb * qb, H, d)[:, :L]


def token_mixer(u, conv_hist, s0, k_past, v_past, w_in, conv_w, a_log, dt_bias, dn_norm, sb_bias, sb_norm, w_o):
    B, L, _ = u.shape
    proj = u @ w_in
    cuts = np.cumsum([QKV_COLS, DN_HEADS * DN_V_DIM, DN_HEADS, DN_HEADS, SB_DIM, SB_DIM]).tolist()
    qkv_dn, z, a, b, q_sb, k_sb, v_sb = jnp.split(proj, cuts, axis=-1)
    qkv_dn, new_conv = causal_short_conv(qkv_dn, conv_hist, conv_w)
    q_dn, k_dn, v_dn = jnp.split(qkv_dn, 3, axis=-1)
    q_dn = l2_normalize(q_dn.reshape(B, L, DN_HEADS, DN_K_DIM)) * (DN_K_DIM ** -0.5)
    k_dn = l2_normalize(k_dn.reshape(B, L, DN_HEADS, DN_K_DIM))
    v_dn = v_dn.reshape(B, L, DN_HEADS, DN_V_DIM)
    g = -jnp.exp(a_log.astype(jnp.float32)) * jax.nn.softplus(a.astype(jnp.float32) + dt_bias.astype(jnp.float32))
    beta = jax.nn.sigmoid(b.astype(jnp.float32))
    o_dn, s_new = gated_delta_rule(q_dn, k_dn, v_dn, g, beta, s0)
    o_dn = rms_norm(o_dn, dn_norm) * jax.nn.silu(z.reshape(B, L, DN_HEADS, DN_V_DIM).astype(jnp.float32))
    q_sb = q_sb.reshape(B, L, SB_HEADS, SB_HEAD_DIM)
    k_sb = k_sb.reshape(B, L, SB_HEADS, SB_HEAD_DIM)
    v_sb = v_sb.reshape(B, L, SB_HEADS, SB_HEAD_DIM)
    k_all = jnp.concatenate([k_past.astype(k_sb.dtype), k_sb], axis=1)
    v_all = jnp.concatenate([v_past.astype(v_sb.dtype), v_sb], axis=1)
    o_sb = rms_norm(stick_breaking_attention(q_sb, k_all, v_all, k_past.shape[1], sb_bias), sb_norm)
    mixed = jnp.concatenate([o_dn.reshape(B, L, DN_HEADS * DN_V_DIM), o_sb.reshape(B, L, SB_DIM)], axis=-1)
    return mixed.astype(u.dtype) @ w_o, k_sb, v_sb, s_new, new_conv


def decoder_layer(x, conv_hist, s0, k_past, v_past, norms, f1_gu, f1_d, w_in, conv_w, a_log, dt_bias,
                  dn_norm, sb_bias, sb_norm, w_o, f2_gu, f2_d):
    h = x + 0.5 * rms_norm(swiglu_ffn(rms_norm(x, norms[0]), f1_gu, f1_d), norms[1])
    mix, k_new, v_new, s_new, conv_new = token_mixer(rms_norm(h, norms[2]), conv_hist, s0, k_past, v_past,
                                                     w_in, conv_w, a_log, dt_bias, dn_norm, sb_bias, sb_norm, w_o)
    h = h + rms_norm(mix, norms[3])
    h = h + 0.5 * rms_norm(swiglu_ffn(rms_norm(h, norms[4]), f2_gu, f2_d), norms[5])
    return h, k_new, v_new, s_new, conv_new


def run_group(x, conv_state, delta_state, past_k, past_v, norms, ffn1_w_gate_up, ffn1_w_down, w_in, conv_w,
              a_log, dt_bias, dn_out_norm, sb_logit_bias, sb_out_norm, w_o, ffn2_w_gate_up, ffn2_w_down):
    ks, vs, ss, cs = [], [], [], []
    for l in range(DEPTH):
        x, k_new, v_new, s_new, c_new = decoder_layer(
            x, conv_state[l], delta_state[l], past_k(l), past_v(l), norms[l], ffn1_w_gate_up[l], ffn1_w_down[l],
            w_in[l], conv_w[l], a_log[l], dt_bias[l], dn_out_norm[l], sb_logit_bias[l], sb_out_norm[l], w_o[l],
            ffn2_w_gate_up[l], ffn2_w_down[l])
        ks.append(k_new); vs.append(v_new); ss.append(s_new); cs.append(c_new)
    return x, jnp.stack(ks, axis=2), jnp.stack(vs, axis=2), jnp.stack(ss, axis=0), jnp.stack(cs, axis=0)


def setup_inputs(seed: int = 0) -> dict:
    key = jax.random.key(seed)
    ks = jax.random.split(key, 21)
    f32 = jnp.float32
    n_pages = PAST_LEN // PAGE_SIZE
    n_used = DEC_BATCH * n_pages
    n_phys = n_used + max(1, n_used // 4)

    def nrm(k, shape, scale):
        return jax.random.normal(k, shape, f32) * scale

    dt = jnp.exp(jax.random.uniform(ks[13], (DEPTH, DN_HEADS), f32, math.log(1e-3), math.log(1e-1)))
    return {
        "x_prompt": nrm(ks[0], (BATCH, SEQ, D_MODEL), 1.0),
        "x_sample": nrm(ks[1], (DEC_BATCH, DEC_SEQ, D_MODEL), 1.0),
        "cache_sb_k": nrm(ks[2], (n_phys, PAGE_SIZE, DEPTH, SB_HEADS, SB_HEAD_DIM), 1.0),
        "cache_sb_v": nrm(ks[3], (n_phys, PAGE_SIZE, DEPTH, SB_HEADS, SB_HEAD_DIM), 1.0),
        "page_table": jax.random.permutation(ks[4], n_phys)[:n_used].reshape(DEC_BATCH, n_pages).astype(jnp.int32),
        "state_delta": nrm(ks[5], (DEPTH, DEC_BATCH, DN_HEADS, DN_K_DIM, DN_V_DIM), 0.1),
        "state_conv": nrm(ks[6], (DEPTH, DEC_BATCH, CONV_W - 1, QKV_COLS), 1.0),
        "norms": 1.0 + nrm(ks[7], (DEPTH, 6, D_MODEL), 0.02),
        "ffn1_w_gate_up": nrm(ks[8], (DEPTH, D_MODEL, 2 * D_FF), D_MODEL ** -0.5),
        "ffn1_w_down": nrm(ks[9], (DEPTH, D_FF, D_MODEL), D_FF ** -0.5),
        "w_in": nrm(ks[10], (DEPTH, D_MODEL, IN_COLS), D_MODEL ** -0.5),
        "conv_w": nrm(ks[11], (DEPTH, CONV_W, QKV_COLS), CONV_W ** -0.5),
        "a_log": jnp.log(jax.random.uniform(ks[12], (DEPTH, DN_HEADS), f32, 1.0, 16.0)),
        "dt_bias": dt + jnp.log(-jnp.expm1(-dt)),
        "dn_out_norm": 1.0 + nrm(ks[14], (DEPTH, DN_V_DIM), 0.02),
        "sb_logit_bias": SB_BIAS_INIT + nrm(ks[19], (DEPTH, SB_HEADS), 0.1),
        "sb_out_norm": 1.0 + nrm(ks[15], (DEPTH, SB_HEAD_DIM), 0.02),
        "w_o": nrm(ks[16], (DEPTH, MIX_DIM, D_MODEL), MIX_DIM ** -0.5),
        "ffn2_w_gate_up": nrm(ks[17], (DEPTH, D_MODEL, 2 * D_FF), D_MODEL ** -0.5),
        "ffn2_w_down": nrm(ks[18], (DEPTH, D_FF, D_MODEL), D_FF ** -0.5),
    }


def reference(x_prompt, x_sample, cache_sb_k, cache_sb_v, page_table, state_delta, state_conv, norms,
              ffn1_w_gate_up, ffn1_w_down, w_in, conv_w, a_log, dt_bias, dn_out_norm, sb_logit_bias, sb_out_norm,
              w_o, ffn2_w_gate_up, ffn2_w_down):
    bp = x_prompt.shape[0]
    zero_conv = jnp.zeros((DEPTH, bp, CONV_W - 1, QKV_COLS), x_prompt.dtype)
    zero_delta = jnp.zeros((DEPTH, bp, DN_HEADS, DN_K_DIM, DN_V_DIM), jnp.float32)
    empty = jnp.zeros((bp, 0, SB_HEADS, SB_HEAD_DIM), x_prompt.dtype)
    y_prompt, k_rows_prompt, v_rows_prompt, delta_prompt, conv_prompt = run_group(
        x_prompt, zero_conv, zero_delta, lambda l: empty, lambda l: empty, norms, ffn1_w_gate_up, ffn1_w_down,
        w_in, conv_w, a_log, dt_bias, dn_out_norm, sb_logit_bias, sb_out_norm, w_o, ffn2_w_gate_up, ffn2_w_down)

    def gather_past(cache, l):
        rows = cache[:, :, l][page_table]
        return rows.reshape(rows.shape[0], rows.shape[1] * rows.shape[2], SB_HEADS, SB_HEAD_DIM)

    y_sample, k_rows_sample, v_rows_sample, delta_sample, conv_sample = run_group(
        x_sample, state_conv, state_delta, lambda l: gather_past(cache_sb_k, l), lambda l: gather_past(cache_sb_v, l),
        norms, ffn1_w_gate_up, ffn1_w_down, w_in, conv_w, a_log, dt_bias, dn_out_norm, sb_logit_bias, sb_out_norm,
        w_o, ffn2_w_gate_up, ffn2_w_down)

    return (y_prompt, y_sample, k_rows_prompt, v_rows_prompt, k_rows_sample, v_rows_sample,
            delta_prompt, delta_sample, conv_prompt, conv_sample)
```

```python
import functools

import jax
import jax.numpy as jnp
from jax import lax
from jax.experimental import pallas as pl
from jax.experimental.pallas import tpu as pltpu

F32 = jnp.float32
BF16 = jnp.bfloat16
HIGHEST = lax.Precision.HIGHEST

EPS = 1e-6
DN_HEADS = 4
DN_DIM = 128
SB_HEADS = 8
SB_DIM = 64
CONV_W = 4
LANES = 128
SUBLANES = 8
VMEM_LIMIT = 56 * 1024 * 1024


def _dot(a, b, precision=None):
    return jnp.dot(a, b, preferred_element_type=F32, precision=precision)


def _dot_nt(a, b):
    return lax.dot_general(a, b, (((1,), (1,)), ((), ())), preferred_element_type=F32)


def _dot_tn(a, b):
    return lax.dot_general(a, b, (((0,), (0,)), ((), ())), preferred_element_type=F32)


def _rms(x, gain):
    ms = jnp.mean(x * x, axis=-1, keepdims=True)
    return x * lax.rsqrt(ms + EPS) * gain


def _sigmoid(x):
    return 1.0 / (1.0 + jnp.exp(-x))


def _softplus(x):
    return jnp.maximum(x, 0.0) + jnp.log(1.0 + jnp.exp(-jnp.abs(x)))


def _params(*sem):
    return pltpu.CompilerParams(dimension_semantics=sem, vmem_limit_bytes=VMEM_LIMIT)


def _ffn_kernel(x_ref, g_ref, wgu_ref, wd_ref, o_ref, acc_ref, *, pre, post, d_ff, chunk):
    x = x_ref[...]
    xn = _rms(x, g_ref[pre:pre + 1, :]).astype(BF16)
    for c in range(d_ff // chunk):
        gate = _dot(xn, wgu_ref[:, c * chunk:(c + 1) * chunk])
        up = _dot(xn, wgu_ref[:, d_ff + c * chunk:d_ff + (c + 1) * chunk])
        h = (gate * _sigmoid(gate) * up).astype(BF16)
        part = _dot(h, wd_ref[c * chunk:(c + 1) * chunk, :])
        if c == 0:
            acc_ref[...] = part
        else:
            acc_ref[...] += part
    o_ref[...] = x + 0.5 * _rms(acc_ref[...], g_ref[post:post + 1, :])


def _ffn(x, norms, w_gu, w_d, pre, post, tm):
    t, d = x.shape
    d_ff = w_d.shape[0]
    const = lambda i: (0, 0)
    return pl.pallas_call(
        functools.partial(_ffn_kernel, pre=pre, post=post, d_ff=d_ff, chunk=256),
        out_shape=jax.ShapeDtypeStruct((t, d), F32),
        grid=(t // tm,),
        in_specs=[pl.BlockSpec((tm, d), lambda i: (i, 0)),
                  pl.BlockSpec(norms.shape, const),
                  pl.BlockSpec(w_gu.shape, const),
                  pl.BlockSpec(w_d.shape, const)],
        out_specs=pl.BlockSpec((tm, d), lambda i: (i, 0)),
        scratch_shapes=[pltpu.VMEM((tm, d), F32)],
        compiler_params=_params("arbitrary"),
        name="ffn",
    )(x, norms, w_gu, w_d)


def _inproj_kernel(h_ref, g_ref, wqkv_ref, wz_ref, wab_ref, wsb_ref,
                   qkv_ref, z_ref, ab_ref, q_ref, k_ref, v_ref, kb_ref, vb_ref, *, sb):
    u = _rms(h_ref[...], g_ref[2:3, :]).astype(BF16)
    qkv_ref[...] = _dot(u, wqkv_ref[...])
    z_ref[...] = _dot(u, wz_ref[...])
    ab_ref[...] = _dot(u, wab_ref[...])
    p = _dot(u, wsb_ref[...])
    k = p[:, sb:2 * sb]
    v = p[:, 2 * sb:]
    q_ref[...] = p[:, :sb]
    k_ref[...] = k
    v_ref[...] = v
    kb_ref[...] = k.astype(BF16)
    vb_ref[...] = v.astype(BF16)


def _inproj(h, norms, w_qkv, w_z, w_ab, w_sb, tm):
    t, d = h.shape
    sb = SB_HEADS * SB_DIM
    const = lambda i: (0, 0)
    row = lambda n: pl.BlockSpec((tm, n), lambda i: (i, 0))
    widths = (w_qkv.shape[1], w_z.shape[1], LANES, sb, sb, sb, sb, sb)
    dtypes = (F32, F32, F32, F32, F32, F32, BF16, BF16)
    return pl.pallas_call(
        functools.partial(_inproj_kernel, sb=sb),
        out_shape=[jax.ShapeDtypeStruct((t, n), dt) for n, dt in zip(widths, dtypes)],
        grid=(t // tm,),
        in_specs=[row(d), pl.BlockSpec(norms.shape, const)]
                 + [pl.BlockSpec(w.shape, const) for w in (w_qkv, w_z, w_ab, w_sb)],
        out_specs=[row(n) for n in widths],
        compiler_params=_params("arbitrary"),
        name="inproj",
    )(h, norms, w_qkv, w_z, w_ab, w_sb)


def _conv_kernel(x_ref, prev_ref, hist_ref, w_ref, q_ref, k_ref, v_ref):
    x = x_ref[...]
    w = w_ref[...]
    prev = jnp.where(pl.program_id(1) == 0, hist_ref[0], prev_ref[...])
    row = lax.broadcasted_iota(jnp.int32, prev.shape, 0)
    y = x * w[CONV_W - 1:CONV_W, :]
    y_head = y[0:SUBLANES]
    for s in range(1, CONV_W):
        tap = w[CONV_W - 1 - s:CONV_W - s, :]
        xs = pltpu.roll(x, s, 0)
        y = y + xs * tap
        head = jnp.where(row < s, pltpu.roll(prev, s, 0), xs[0:SUBLANES])
        y_head = y_head + head * tap

    def finish(y, rows):
        y = y * _sigmoid(y)
        nq = DN_HEADS * DN_DIM
        for h in range(DN_HEADS):
            for ref, off, scale in ((q_ref, 0, DN_DIM ** -0.5), (k_ref, nq, 1.0)):
                t = y[:, off + h * DN_DIM:off + (h + 1) * DN_DIM]
                ss = jnp.sum(t * t, axis=-1, keepdims=True)
                ref[rows, h * DN_DIM:(h + 1) * DN_DIM] = t * (lax.rsqrt(ss + EPS) * scale)
        v_ref[rows, :] = y[:, 2 * nq:]

    finish(y, slice(None))
    finish(y_head, slice(0, SUBLANES))


def _conv(x, hist8, w, n_seq, tm):
    t, c = x.shape
    nt = t // n_seq // tm
    blocks8 = tm // SUBLANES
    n = DN_HEADS * DN_DIM
    out = pl.BlockSpec((tm, n), lambda b, i: (b * nt + i, 0))
    return pl.pallas_call(
        _conv_kernel,
        out_shape=[jax.ShapeDtypeStruct((t, n), F32)] * 3,
        grid=(n_seq, nt),
        in_specs=[pl.BlockSpec((tm, c), lambda b, i: (b * nt + i, 0)),
                  pl.BlockSpec((SUBLANES, c), lambda b, i: (jnp.maximum((b * nt + i) * blocks8 - 1, 0), 0)),
                  pl.BlockSpec((1, SUBLANES, c), lambda b, i: (b, 0, 0)),
                  pl.BlockSpec(w.shape, lambda b, i: (0, 0))],
        out_specs=[out, out, out],
        compiler_params=_params("arbitrary", "arbitrary"),
        name="conv",
    )(x, x, hist8, w)


def _delta_kernel(q_ref, k_ref, v_ref, ab_ref, z_ref, s0_ref, alog_ref, dtb_ref, gain_ref,
                  o_ref, s_ref, *, chunk, n_chunks, valid_rows):
    c = chunk

    @pl.when(pl.program_id(1) == 0)
    def _():
        s_ref[...] = s0_ref[...]

    ri = lax.broadcasted_iota(jnp.int32, (c, c), 0)
    ci = lax.broadcasted_iota(jnp.int32, (c, c), 1)
    incl = ri >= ci
    strict = ri > ci
    ltri = incl.astype(F32)
    utri = (ri <= ci).astype(F32)
    eye = (ri == ci).astype(F32)
    ones = jnp.ones((c, c), F32)
    neg_a = -jnp.exp(alog_ref[...])
    dtb = dtb_ref[...]
    gain = gain_ref[...]
    n_doublings = max(c.bit_length() - 2, 0)

    def chunk_body(ch, carry):
        r0 = pl.multiple_of(ch * c, c)
        rows = pl.ds(r0, c)
        gb = ab_ref[rows, :]
        g_all = neg_a * _softplus(gb + dtb)
        beta_all = _sigmoid(gb)
        if valid_rows < c:
            live = lax.broadcasted_iota(jnp.int32, g_all.shape, 0) < valid_rows
            g_all = jnp.where(live, g_all, 0.0)
            beta_all = jnp.where(live, beta_all, 0.0)
        gcum_all = _dot(ltri, g_all, HIGHEST)
        for h in range(DN_HEADS):
            cols = slice(h * DN_DIM, (h + 1) * DN_DIM)
            g_col = g_all[:, h:h + 1]
            beta = beta_all[:, DN_HEADS + h:DN_HEADS + h + 1]
            gcum = gcum_all[:, h:h + 1]
            gcum_row = _dot(ones, jnp.broadcast_to(g_col, (c, c)) * utri, HIGHEST)
            decay = jnp.where(incl, jnp.exp(jnp.minimum(gcum - gcum_row, 0.0)), 0.0)
            q = q_ref[rows, cols]
            k = k_ref[rows, cols]
            v = v_ref[rows, cols]
            kb = k * beta
            k16 = k.astype(BF16)
            m = jnp.where(strict, _dot_nt(kb.astype(BF16), k16) * decay, 0.0)
            qk = _dot_nt(q.astype(BF16), k16) * decay
            pw = -m
            inv = eye + pw
            for _ in range(n_doublings):
                pw = _dot(pw, pw, HIGHEST)
                inv = inv + _dot(inv, pw, HIGHEST)
            rhs = jnp.concatenate([v * beta, kb * jnp.exp(gcum)], axis=1)
            sol = _dot(inv, rhs, HIGHEST)
            u_base = sol[:, :DN_DIM]
            w_state = sol[:, DN_DIM:]
            s = s_ref[0, h]
            s16 = s.astype(BF16)
            u = u_base - _dot(w_state.astype(BF16), s16)
            u16 = u.astype(BF16)
            o = _dot((q * jnp.exp(gcum)).astype(BF16), s16) + _dot(qk.astype(BF16), u16)
            g_last = gcum[c - 1:c, :]
            k_dec = k * jnp.exp(g_last - gcum)
            s_ref[0, h] = s * jnp.exp(g_last) + _dot_tn(k_dec.astype(BF16), u16)
            zz = z_ref[rows, cols]
            o_ref[rows, cols] = _rms(o, gain) * (zz * _sigmoid(zz))
        return carry

    lax.fori_loop(0, n_chunks, chunk_body, 0)


def _delta(q, k, v, ab, z, s0, a_log, dt_bias, gain, n_seq, chunk, n_chunks, valid_rows):
    t, n = q.shape
    rows = chunk * n_chunks
    steps = t // n_seq // rows
    blk = lambda w: pl.BlockSpec((rows, w), lambda b, i: (b * steps + i, 0))
    state = pl.BlockSpec((1,) + s0.shape[1:], lambda b, i: (b, 0, 0, 0))
    vec = pl.BlockSpec((1, LANES), lambda b, i: (0, 0))
    return pl.pallas_call(
        functools.partial(_delta_kernel, chunk=chunk, n_chunks=n_chunks, valid_rows=valid_rows),
        out_shape=[jax.ShapeDtypeStruct((t, n), F32), jax.ShapeDtypeStruct(s0.shape, F32)],
        grid=(n_seq, steps),
        in_specs=[blk(n), blk(n), blk(n), blk(LANES), blk(n), state, vec, vec, vec],
        out_specs=[blk(n), state],
        compiler_params=_params("arbitrary", "arbitrary"),
        name="delta",
    )(q, k, v, ab, z, s0, a_log, dt_bias, gain)


def _sb_weights(z, later, uo, valid=None):
    n = z.shape[1]
    sp = _softplus(z)
    if valid is not None:
        sp = jnp.where(valid, sp, 0.0)
    hi = sp.astype(BF16)
    lo = (sp - hi.astype(F32)).astype(BF16)
    sums = _dot(jnp.concatenate([hi, lo], axis=1), uo)
    a = jnp.exp(z - sums[:, :n] - later)
    if valid is not None:
        a = jnp.where(valid, a, 0.0)
    return a.astype(BF16), later + sums[:, n:]


def _suffix_sum_matrix(n):
    j = lax.broadcasted_iota(jnp.int32, (n, n), 0)
    s = lax.broadcasted_iota(jnp.int32, (n, n), 1)
    half = jnp.concatenate([(j >= s).astype(BF16), jnp.ones((n, n), BF16)], axis=1)
    return jnp.concatenate([half, half], axis=0)


def _sbp_kernel(bias_ref, q_ref, k_ref, v_ref, gain_ref, uo_ref, bd_ref, o_ref, *, tq):
    pair = pl.program_id(1)
    i = pl.program_id(2)
    lane = lax.broadcasted_iota(jnp.int32, (tq, LANES), 1)
    first = lane < SB_DIM
    q = q_ref[0] * (SB_DIM ** -0.5)
    q_heads = (jnp.where(first, q, 0.0).astype(BF16), jnp.where(first, 0.0, q).astype(BF16))
    bias = (bias_ref[2 * pair], bias_ref[2 * pair + 1])
    uo = uo_ref[...]
    causal = lax.broadcasted_iota(jnp.int32, (tq, tq), 1) < lax.broadcasted_iota(jnp.int32, (tq, tq), 0)

    def block(j, carry, valid):
        rows = pl.ds(pl.multiple_of(j * tq, tq), tq)
        ks = k_ref[0, rows, :]
        vs = v_ref[0, rows, :]
        out = []
        for hh in range(2):
            later, acc = carry[hh]
            z = _dot_nt(q_heads[hh], ks) + bias[hh]
            a, later = _sb_weights(z, later, uo, valid)
            out.append((later, acc + _dot(a, vs)))
        return tuple(out)

    zero = jnp.zeros((tq, LANES), F32)
    carry = block(i, ((zero, zero), (zero, zero)), causal)
    carry = lax.fori_loop(0, i, lambda t, c: block(i - 1 - t, c, None), carry)
    o = jnp.where(first, carry[0][1], carry[1][1])
    ss = _dot(o * o, bd_ref[...], HIGHEST)
    o_ref[0] = o * lax.rsqrt(ss * (1.0 / SB_DIM) + EPS) * gain_ref[...]


def _sb_prompt(q, k16, v16, bias, gain2, n_seq, tq):
    t, n = q.shape
    l = t // n_seq
    pairs = n // LANES
    q3 = q.reshape(n_seq, l, n)
    k3 = k16.reshape(n_seq, l, n)
    v3 = v16.reshape(n_seq, l, n)
    seg = jnp.arange(LANES) // SB_DIM
    bd = (seg[:, None] == seg[None, :]).astype(F32)
    uo = _suffix_sum_matrix(tq)
    kv = pl.BlockSpec((1, l, LANES), lambda b, p, i: (b, 0, p))
    const = lambda b, p, i: (0, 0)
    out = pl.pallas_call(
        functools.partial(_sbp_kernel, tq=tq),
        out_shape=jax.ShapeDtypeStruct((n_seq, l, n), F32),
        grid=(n_seq, pairs, l // tq),
        in_specs=[pl.BlockSpec(memory_space=pltpu.SMEM),
                  pl.BlockSpec((1, tq, LANES), lambda b, p, i: (b, i, p)),
                  kv, kv,
                  pl.BlockSpec((1, LANES), const),
                  pl.BlockSpec(uo.shape, const),
                  pl.BlockSpec(bd.shape, const)],
        out_specs=pl.BlockSpec((1, tq, LANES), lambda b, p, i: (b, i, p)),
        compiler_params=_params("arbitrary", "arbitrary", "arbitrary"),
        name="sb_prompt",
    )(bias, q3, k3, v3, gain2, uo, bd)
    return out.reshape(t, n)


def _sbd_kernel(pt_ref, q_ref, bias_ref, knew_ref, vnew_ref, gain_ref, uo_ref, *rest, pages, n_new):
    k_pages = rest[:pages]
    v_pages = rest[pages:2 * pages]
    o_ref = rest[2 * pages]
    later_ref, acc_ref, kpad_ref, vpad_ref = rest[2 * pages + 1:]
    step = pl.program_id(1)
    n_rows, width = acc_ref.shape
    page = later_ref.shape[1]
    own_head = (lax.broadcasted_iota(jnp.int32, (n_rows, width), 1) // SB_DIM
                == lax.broadcasted_iota(jnp.int32, (n_rows, width), 0) % SB_HEADS)
    q = jnp.where(own_head, q_ref[0] * (SB_DIM ** -0.5), 0.0).astype(BF16)
    bias = bias_ref[...]
    uo = uo_ref[...]

    def block(z, weighted_sum, valid):
        a, later = _sb_weights(z + bias, later_ref[...], uo, valid)
        later_ref[...] = later
        acc_ref[...] += weighted_sum(a)

    @pl.when(step == 0)
    def _():
        later_ref[...] = jnp.zeros_like(later_ref)
        acc_ref[...] = jnp.zeros_like(acc_ref)
        kpad_ref[...] = jnp.zeros_like(kpad_ref)
        vpad_ref[...] = jnp.zeros_like(vpad_ref)
        kpad_ref[0:n_new, :] = knew_ref[0]
        vpad_ref[0:n_new, :] = vnew_ref[0]
        key = lax.broadcasted_iota(jnp.int32, (n_rows, page), 1)
        tok = lax.broadcasted_iota(jnp.int32, (n_rows, page), 0) // SB_HEADS
        vs = vpad_ref[...].astype(BF16)
        block(_dot_nt(q, kpad_ref[...].astype(BF16)), lambda a: _dot(a, vs), key < tok)

    for i in reversed(range(pages)):
        vt = v_pages[i][0, 0].astype(BF16)
        block(_dot(q, k_pages[i][0, 0].astype(BF16)), lambda a, vt=vt: _dot_nt(a, vt), None)

    @pl.when(step == pl.num_programs(1) - 1)
    def _():
        o = jnp.where(own_head, acc_ref[...], 0.0)
        ss = jnp.sum(o * o, axis=-1, keepdims=True)
        o = o * lax.rsqrt(ss * (1.0 / SB_DIM) + EPS)
        o = jnp.sum(o.reshape(n_rows // SB_HEADS, SB_HEADS, width), axis=1)
        o_ref[0] = o * gain_ref[...]


def _sb_decode(q, k_new, v_new, cache_k, cache_v, page_table, layer, bias, gain, pages):
    n_seq, n_new, n = q.shape
    page = cache_k.shape[3]
    n_pages = page_table.shape[1]
    steps = n_pages // pages
    n_rows = n_new * SB_HEADS
    q_rep = jnp.repeat(q, SB_HEADS, axis=1)
    bias_col = jnp.tile(bias, n_new).reshape(n_rows, 1)
    gain_row = jnp.tile(gain, SB_HEADS).reshape(1, n)
    uo = _suffix_sum_matrix(page)

    def page_spec(i):
        return pl.BlockSpec((1, 1, n, page),
                            lambda b, s, pt: (pt[b, (steps - 1 - s) * pages + i], layer, 0, 0))

    per_seq = lambda r: pl.BlockSpec((1, r, n), lambda b, s, pt: (b, 0, 0))
    const = lambda b, s, pt: (0, 0)
    grid_spec = pltpu.PrefetchScalarGridSpec(
        num_scalar_prefetch=1,
        grid=(n_seq, steps),
        in_specs=[per_seq(n_rows),
                  pl.BlockSpec(bias_col.shape, const),
                  per_seq(n_new), per_seq(n_new),
                  pl.BlockSpec(gain_row.shape, const),
                  pl.BlockSpec(uo.shape, const)]
                 + [page_spec(i) for i in range(pages)] * 2,
        out_specs=per_seq(n_new),
        scratch_shapes=[pltpu.VMEM((n_rows, page), F32), pltpu.VMEM((n_rows, n), F32),
                        pltpu.VMEM((page, n), F32), pltpu.VMEM((page, n), F32)],
    )
    return pl.pallas_call(
        functools.partial(_sbd_kernel, pages=pages, n_new=n_new),
        out_shape=jax.ShapeDtypeStruct((n_seq, n_new, n), F32),
        grid_spec=grid_spec,
        compiler_params=_params("arbitrary", "arbitrary"),
        name="sb_decode",
    )(page_table, q_rep, bias_col, k_new, v_new, gain_row, uo,
      *([cache_k] * pages), *([cache_v] * pages))


def _outproj_kernel(a_ref, b_ref, h_ref, g_ref, wa_ref, wb_ref, o_ref):
    mix = _dot(a_ref[...].astype(BF16), wa_ref[...]) + _dot(b_ref[...].astype(BF16), wb_ref[...])
    o_ref[...] = h_ref[...] + _rms(mix, g_ref[3:4, :])


def _outproj(a, b, h, norms, w_a, w_b, tm):
    t, d = h.shape
    const = lambda i: (0, 0)
    row = lambda n: pl.BlockSpec((tm, n), lambda i: (i, 0))
    return pl.pallas_call(
        _outproj_kernel,
        out_shape=jax.ShapeDtypeStruct((t, d), F32),
        grid=(t // tm,),
        in_specs=[row(a.shape[1]), row(b.shape[1]), row(d), pl.BlockSpec(norms.shape, const),
                  pl.BlockSpec(w_a.shape, const), pl.BlockSpec(w_b.shape, const)],
        out_specs=row(d),
        compiler_params=_params("arbitrary"),
        name="outproj",
    )(a, b, h, norms, w_a, w_b)


def _pad_lanes(v):
    return jnp.pad(v.astype(F32), (0, LANES - v.shape[0])).reshape(1, LANES)


def _run_group(x, weights, *, tm, past):
    n_seq, l, d = x.shape
    depth = weights["norms"].shape[0]
    nq = DN_HEADS * DN_DIM
    sb = SB_HEADS * SB_DIM
    h = x.reshape(n_seq * l, d)
    ks, vs, ss, cs = [], [], [], []
    for layer in range(depth):
        norms = weights["norms"][layer]
        w_in = weights["w_in"][layer].astype(BF16)
        w_qkv = w_in[:, :3 * nq]
        w_z = w_in[:, 3 * nq:4 * nq]
        w_ab = jnp.pad(w_in[:, 4 * nq:4 * nq + 2 * DN_HEADS], ((0, 0), (0, LANES - 2 * DN_HEADS)))
        w_sb = w_in[:, 4 * nq + 2 * DN_HEADS:]
        w_o = weights["w_o"][layer].astype(BF16)
        a_log = _pad_lanes(weights["a_log"][layer])
        dt_bias = _pad_lanes(weights["dt_bias"][layer])
        dn_gain = weights["dn_out_norm"][layer].reshape(1, DN_DIM)
        sb_bias = weights["sb_logit_bias"][layer]
        sb_gain = weights["sb_out_norm"][layer]

        h = _ffn(h, norms, weights["ffn1_w_gate_up"][layer].astype(BF16),
                 weights["ffn1_w_down"][layer].astype(BF16), 0, 1, tm)
        qkv, z, ab, q_sb, k_sb, v_sb, k16, v16 = _inproj(h, norms, w_qkv, w_z, w_ab, w_sb, tm)
        conv_w = weights["conv_w"][layer]
        if past is None:
            hist8 = jnp.zeros((n_seq, SUBLANES, 3 * nq), F32)
            q_dn, k_dn, v_dn = _conv(qkv, hist8, conv_w, n_seq, tm)
            s0 = jnp.zeros((n_seq, DN_HEADS, DN_DIM, DN_DIM), F32)
            o_dn, s_new = _delta(q_dn, k_dn, v_dn, ab, z, s0, a_log, dt_bias, dn_gain,
                                 n_seq, chunk=64, n_chunks=8, valid_rows=64)
            o_sb = _sb_prompt(q_sb, k16, v16, sb_bias, jnp.tile(sb_gain, 2).reshape(1, LANES), n_seq, tq=128)
            conv_new = qkv.reshape(n_seq, l, 3 * nq)[:, l - (CONV_W - 1):]
        else:
            conv_state, delta_state, cache_k, cache_v, page_table = past
            qkv3 = qkv.reshape(n_seq, l, 3 * nq)
            hist = conv_state[layer]
            tile = jnp.concatenate(
                [hist, qkv3, jnp.zeros((n_seq, SUBLANES - l - hist.shape[1], 3 * nq), F32)], axis=1)
            tile = tile.reshape(n_seq * SUBLANES, 3 * nq)
            conv_out = _conv(tile, jnp.zeros((n_seq, SUBLANES, 3 * nq), F32), conv_w, n_seq, SUBLANES)
            first = hist.shape[1]

            def chunk8(a, width):
                a = a.reshape(n_seq, SUBLANES, width)[:, first:first + l]
                return jnp.pad(a, ((0, 0), (0, SUBLANES - l), (0, 0))).reshape(n_seq * SUBLANES, width)

            def pad8(a):
                a = a.reshape(n_seq, l, a.shape[1])
                return jnp.pad(a, ((0, 0), (0, SUBLANES - l), (0, 0))).reshape(n_seq * SUBLANES, -1)

            q_dn, k_dn, v_dn = (chunk8(a, nq) for a in conv_out)
            o_dn8, s_new = _delta(q_dn, k_dn, v_dn, pad8(ab), pad8(z), delta_state[layer], a_log, dt_bias,
                                  dn_gain, n_seq, chunk=SUBLANES, n_chunks=1, valid_rows=l)
            o_dn = o_dn8.reshape(n_seq, SUBLANES, nq)[:, :l].reshape(n_seq * l, nq)
            o_sb = _sb_decode(q_sb.reshape(n_seq, l, sb), k_sb.reshape(n_seq, l, sb), v_sb.reshape(n_seq, l, sb),
                              cache_k, cache_v, page_table, layer, sb_bias, sb_gain, pages=8)
            o_sb = o_sb.reshape(n_seq * l, sb)
            conv_new = jnp.concatenate([hist, qkv3], axis=1)[:, -(CONV_W - 1):]
        h = _outproj(o_dn, o_sb, h, norms, w_o[:nq], w_o[nq:], tm)
        h = _ffn(h, norms, weights["ffn2_w_gate_up"][layer].astype(BF16),
                 weights["ffn2_w_down"][layer].astype(BF16), 4, 5, tm)
        ks.append(k_sb.reshape(n_seq, l, SB_HEADS, SB_DIM))
        vs.append(v_sb.reshape(n_seq, l, SB_HEADS, SB_DIM))
        ss.append(s_new)
        cs.append(conv_new)
    return (h.reshape(n_seq, l, d), jnp.stack(ks, axis=2), jnp.stack(vs, axis=2),
            jnp.stack(ss, axis=0), jnp.stack(cs, axis=0))


def kernel(x_prompt, x_sample, cache_sb_k, cache_sb_v, page_table, state_delta, state_conv, norms,
           ffn1_w_gate_up, ffn1_w_down, w_in, conv_w, a_log, dt_bias, dn_out_norm, sb_logit_bias, sb_out_norm,
           w_o, ffn2_w_gate_up, ffn2_w_down):
    weights = dict(norms=norms, ffn1_w_gate_up=ffn1_w_gate_up, ffn1_w_down=ffn1_w_down, w_in=w_in,
                   conv_w=conv_w, a_log=a_log, dt_bias=dt_bias, dn_out_norm=dn_out_norm,
                   sb_logit_bias=sb_logit_bias, sb_out_norm=sb_out_norm, w_o=w_o,
                   ffn2_w_gate_up=ffn2_w_gate_up, ffn2_w_down=ffn2_w_down)
    n_phys, page, depth = cache_sb_k.shape[:3]
    cache_k = jnp.transpose(cache_sb_k, (0, 2, 3, 4, 1)).reshape(n_phys, depth, -1, page)
    cache_v = jnp.transpose(cache_sb_v, (0, 2, 3, 4, 1)).reshape(n_phys, depth, -1, page)
    y_p, k_p, v_p, d_p, c_p = _run_group(x_prompt, weights, tm=512, past=None)
    n_tok = x_sample.shape[0] * x_sample.shape[1]
    y_s, k_s, v_s, d_s, c_s = _run_group(x_sample, weights, tm=n_tok,
                                         past=(state_conv, state_delta, cache_k, cache_v, page_table))
    return (y_p, y_s, k_p, v_p, k_s, v_s, d_p, d_s, c_p, c_s)
```

```python
import functools

import jax
import jax.numpy as jnp
from jax import lax
from jax.experimental import pallas as pl
from jax.experimental.pallas import tpu as pltpu

F32 = jnp.float32
BF16 = jnp.bfloat16
HIGHEST = lax.Precision.HIGHEST

EPS = 1e-6
DN_HEADS = 4
DN_DIM = 128
SB_HEADS = 8
SB_DIM = 64
CONV_W = 4
LANES = 128
SUBLANES = 8
VMEM_LIMIT = 56 * 1024 * 1024
MXU_DIM = 256
SUFFIX_SUM_WIDTH = MXU_DIM
LOG2E = 1.4426950408889634
SAMPLE_CHUNK = 16


def _dot(a, b, precision=None):
    return jnp.dot(a, b, preferred_element_type=F32, precision=precision)


def _dot_nt(a, b):
    return lax.dot_general(a, b, (((1,), (1,)), ((), ())), preferred_element_type=F32)


def _dot_tn(a, b):
    return lax.dot_general(a, b, (((0,), (0,)), ((), ())), preferred_element_type=F32)


def _rms(x, gain):
    ms = jnp.mean(x * x, axis=-1, keepdims=True)
    return x * lax.rsqrt(ms + EPS) * gain


def _sigmoid(x):
    return 1.0 / (1.0 + jnp.exp(-x))


def _softplus(x):
    return jnp.maximum(x, 0.0) + jnp.log(1.0 + jnp.exp(-jnp.abs(x)))


def _params(*sem):
    return pltpu.CompilerParams(dimension_semantics=sem, vmem_limit_bytes=VMEM_LIMIT)


def _ffn_kernel(x_ref, g_ref, wgu_ref, wd_ref, o_ref, acc_ref, *, pre, post, d_ff, chunk):
    x = x_ref[...]
    xn = _rms(x, g_ref[pre:pre + 1, :]).astype(BF16)
    for c in range(d_ff // chunk):
        gate = _dot(xn, wgu_ref[:, c * chunk:(c + 1) * chunk])
        up = _dot(xn, wgu_ref[:, d_ff + c * chunk:d_ff + (c + 1) * chunk])
        h = (gate * _sigmoid(gate) * up).astype(BF16)
        part = _dot(h, wd_ref[c * chunk:(c + 1) * chunk, :])
        if c == 0:
            acc_ref[...] = part
        else:
            acc_ref[...] += part
    o_ref[...] = x + 0.5 * _rms(acc_ref[...], g_ref[post:post + 1, :])


def _ffn(x, norms, w_gu, w_d, pre, post, tm):
    t, d = x.shape
    d_ff = w_d.shape[0]
    const = lambda i: (0, 0)
    return pl.pallas_call(
        functools.partial(_ffn_kernel, pre=pre, post=post, d_ff=d_ff, chunk=256),
        out_shape=jax.ShapeDtypeStruct((t, d), F32),
        grid=(t // tm,),
        in_specs=[pl.BlockSpec((tm, d), lambda i: (i, 0)),
                  pl.BlockSpec(norms.shape, const),
                  pl.BlockSpec(w_gu.shape, const),
                  pl.BlockSpec(w_d.shape, const)],
        out_specs=pl.BlockSpec((tm, d), lambda i: (i, 0)),
        scratch_shapes=[pltpu.VMEM((tm, d), F32)],
        compiler_params=_params("arbitrary"),
        name="ffn",
    )(x, norms, w_gu, w_d)


def _inproj_kernel(h_ref, g_ref, wqkv_ref, wz_ref, wab_ref, wsb_ref,
                   qkv_ref, z_ref, ab_ref, q_ref, k_ref, v_ref, kb_ref, vb_ref, *, sb):
    u = _rms(h_ref[...], g_ref[2:3, :]).astype(BF16)
    qkv_ref[...] = _dot(u, wqkv_ref[...])
    z_ref[...] = _dot(u, wz_ref[...])
    ab_ref[...] = _dot(u, wab_ref[...])
    p = _dot(u, wsb_ref[...])
    k = p[:, sb:2 * sb]
    v = p[:, 2 * sb:]
    q_ref[...] = p[:, :sb]
    k_ref[...] = k
    v_ref[...] = v
    kb_ref[...] = k.astype(BF16)
    vb_ref[...] = v.astype(BF16)


def _inproj(h, norms, w_qkv, w_z, w_ab, w_sb, tm):
    t, d = h.shape
    sb = SB_HEADS * SB_DIM
    const = lambda i: (0, 0)
    row = lambda n: pl.BlockSpec((tm, n), lambda i: (i, 0))
    widths = (w_qkv.shape[1], w_z.shape[1], LANES, sb, sb, sb, sb, sb)
    dtypes = (F32, F32, F32, F32, F32, F32, BF16, BF16)
    return pl.pallas_call(
        functools.partial(_inproj_kernel, sb=sb),
        out_shape=[jax.ShapeDtypeStruct((t, n), dt) for n, dt in zip(widths, dtypes)],
        grid=(t // tm,),
        in_specs=[row(d), pl.BlockSpec(norms.shape, const)]
                 + [pl.BlockSpec(w.shape, const) for w in (w_qkv, w_z, w_ab, w_sb)],
        out_specs=[row(n) for n in widths],
        compiler_params=_params("arbitrary"),
        name="inproj",
    )(h, norms, w_qkv, w_z, w_ab, w_sb)


def _conv_kernel(x_ref, prev_ref, hist_ref, w_ref, q_ref, k_ref, v_ref):
    x = x_ref[...]
    w = w_ref[...]
    prev = jnp.where(pl.program_id(1) == 0, hist_ref[0], prev_ref[...])
    row = lax.broadcasted_iota(jnp.int32, prev.shape, 0)
    y = x * w[CONV_W - 1:CONV_W, :]
    y_head = y[0:SUBLANES]
    for s in range(1, CONV_W):
        tap = w[CONV_W - 1 - s:CONV_W - s, :]
        xs = pltpu.roll(x, s, 0)
        y = y + xs * tap
        head = jnp.where(row < s, pltpu.roll(prev, s, 0), xs[0:SUBLANES])
        y_head = y_head + head * tap

    def finish(y, rows):
        y = y * _sigmoid(y)
        nq = DN_HEADS * DN_DIM
        for h in range(DN_HEADS):
            for ref, off, scale in ((q_ref, 0, DN_DIM ** -0.5), (k_ref, nq, 1.0)):
                t = y[:, off + h * DN_DIM:off + (h + 1) * DN_DIM]
                ss = jnp.sum(t * t, axis=-1, keepdims=True)
                ref[rows, h * DN_DIM:(h + 1) * DN_DIM] = t * (lax.rsqrt(ss + EPS) * scale)
        v_ref[rows, :] = y[:, 2 * nq:]

    finish(y, slice(None))
    finish(y_head, slice(0, SUBLANES))


def _conv(x, hist8, w, n_seq, tm):
    t, c = x.shape
    nt = t // n_seq // tm
    blocks8 = tm // SUBLANES
    n = DN_HEADS * DN_DIM
    out = pl.BlockSpec((tm, n), lambda b, i: (b * nt + i, 0))
    return pl.pallas_call(
        _conv_kernel,
        out_shape=[jax.ShapeDtypeStruct((t, n), F32)] * 3,
        grid=(n_seq, nt),
        in_specs=[pl.BlockSpec((tm, c), lambda b, i: (b * nt + i, 0)),
                  pl.BlockSpec((SUBLANES, c), lambda b, i: (jnp.maximum((b * nt + i) * blocks8 - 1, 0), 0)),
                  pl.BlockSpec((1, SUBLANES, c), lambda b, i: (b, 0, 0)),
                  pl.BlockSpec(w.shape, lambda b, i: (0, 0))],
        out_specs=[out, out, out],
        compiler_params=_params("arbitrary", "arbitrary"),
        name="conv",
    )(x, x, hist8, w)


def _split(x, terms):
    parts = []
    for _ in range(terms):
        p = x.astype(BF16)
        parts.append(p)
        x = x - p.astype(F32)
    return parts


def _dot_left01(a01, b):
    return sum(_dot(a01, p) for p in _split(b, 3))


def _dot_split(a, b):
    a_hi, a_lo = _split(a, 2)
    b_hi, b_lo = _split(b, 2)
    return _dot(a_hi, b_hi) + _dot(a_lo, b_hi) + _dot(a_hi, b_lo)


def _delta_kernel(q_ref, k_ref, v_ref, ab_ref, z_ref, s0_ref, alog_ref, dtb_ref, gain_ref,
                  o_ref, s_ref, u_ref, w_ref, qk_ref, qe_ref, kd_ref, decay_last_ref,
                  *, chunk, n_chunks, valid_rows):
    c = chunk

    @pl.when(pl.program_id(1) == 0)
    def _():
        s_ref[...] = s0_ref[...]

    r = DN_HEADS * c
    ri = lax.broadcasted_iota(jnp.int32, (r, r), 0)
    ci = lax.broadcasted_iota(jnp.int32, (r, r), 1)
    same_head = (ri // c) == (ci // c)
    incl = same_head & (ri >= ci)
    strict = same_head & (ri > ci)
    eye = ri == ci
    ones = jnp.ones((r, r), BF16)
    ltri = (lax.broadcasted_iota(jnp.int32, (c, c), 0) >= lax.broadcasted_iota(jnp.int32, (c, c), 1)).astype(BF16)
    neg_a = -jnp.exp(alog_ref[...])
    dtb = dtb_ref[...]
    gain = gain_ref[...]
    n_doublings = max(c.bit_length() - 1, 0)
    heads = range(DN_HEADS)

    def head_cols(h):
        return slice(h * DN_DIM, (h + 1) * DN_DIM)

    def stacked_rows(ch):
        return pl.ds(pl.multiple_of(ch * r, r), r)

    def local(ch):
        rows = pl.ds(pl.multiple_of(ch * c, c), c)
        gb = ab_ref[rows, :]
        g_all = neg_a * _softplus(gb + dtb)
        beta_all = _sigmoid(gb)
        if valid_rows < c:
            live = lax.broadcasted_iota(jnp.int32, g_all.shape, 0) < valid_rows
            g_all = jnp.where(live, g_all, 0.0)
            beta_all = jnp.where(live, beta_all, 0.0)
        gcum_all = _dot_left01(ltri, g_all)
        decay_last_ref[pl.ds(pl.multiple_of(ch * SUBLANES, SUBLANES), SUBLANES), :] = jnp.broadcast_to(
            jnp.exp(gcum_all[c - 1:c, :]), (SUBLANES, LANES))
        beta = jnp.concatenate([beta_all[:, DN_HEADS + h:DN_HEADS + h + 1] for h in heads], axis=0)
        gcum = jnp.concatenate([gcum_all[:, h:h + 1] for h in heads], axis=0)
        g_last = jnp.concatenate([jnp.broadcast_to(gcum_all[c - 1:c, h:h + 1], (c, 1)) for h in heads], axis=0)
        gcum_cols = jnp.broadcast_to(gcum, (r, r))
        gcum_rows = _dot_left01(ones, jnp.where(eye, gcum_cols, 0.0))
        decay = jnp.where(incl, jnp.exp(jnp.minimum(gcum_cols - gcum_rows, 0.0)), 0.0)
        q = jnp.concatenate([q_ref[rows, head_cols(h)] for h in heads], axis=0)
        k = jnp.concatenate([k_ref[rows, head_cols(h)] for h in heads], axis=0)
        v = jnp.concatenate([v_ref[rows, head_cols(h)] for h in heads], axis=0)
        kb = k * beta
        k16 = k.astype(BF16)
        m = jnp.where(strict, _dot_nt(kb.astype(BF16), k16) * decay, 0.0)
        out = stacked_rows(ch)
        qk_ref[out, :] = (_dot_nt(q.astype(BF16), k16) * decay).astype(BF16)
        inv = jnp.where(eye, 1.0, 0.0) - m
        pw = _dot_split(m, m)
        for step in range(1, n_doublings):
            inv = inv + _dot_split(inv, pw)
            if step + 1 < n_doublings:
                pw = _dot_split(pw, pw)
        e_gcum = jnp.exp(gcum)
        sol = _dot_split(inv, jnp.concatenate([v * beta, kb * e_gcum], axis=1))
        u_ref[out, :] = sol[:, :DN_DIM]
        w_ref[out, :] = sol[:, DN_DIM:].astype(BF16)
        qe_ref[out, :] = (q * e_gcum).astype(BF16)
        kd_ref[out, :] = (k * jnp.exp(g_last - gcum)).astype(BF16)

    if n_chunks % 2 == 0:
        def local_pair(i, carry):
            local(2 * i)
            local(2 * i + 1)
            return carry
        lax.fori_loop(0, n_chunks // 2, local_pair, 0)
    else:
        for ch in range(n_chunks):
            local(ch)

    def scan(ch, carry):
        rows = pl.ds(pl.multiple_of(ch * c, c), c)
        decay_last = decay_last_ref[pl.ds(pl.multiple_of(ch * SUBLANES, SUBLANES), SUBLANES), :]
        s = [s_ref[0, h] for h in heads]
        s16 = [x.astype(BF16) for x in s]
        head_rows = [pl.ds(pl.multiple_of(ch * r + h * c, c), c) for h in heads]
        u16 = [(u_ref[head_rows[h], :] - _dot(w_ref[head_rows[h], :], s16[h])).astype(BF16) for h in heads]
        u16_all = jnp.concatenate(u16, axis=0)
        for h in heads:
            o = _dot(qe_ref[head_rows[h], :], s16[h]) + _dot(qk_ref[head_rows[h], :], u16_all)
            s_ref[0, h] = s[h] * decay_last[0:1, h:h + 1] + _dot_tn(kd_ref[head_rows[h], :], u16[h])
            zz = z_ref[rows, head_cols(h)]
            o_ref[rows, head_cols(h)] = _rms(o, gain) * (zz * _sigmoid(zz))
        return carry

    lax.fori_loop(0, n_chunks, scan, 0)


def _delta(q, k, v, ab, z, s0, a_log, dt_bias, gain, n_seq, chunk, n_chunks, valid_rows):
    t, n = q.shape
    rows = chunk * n_chunks
    stacked = DN_HEADS * rows
    steps = t // n_seq // rows
    blk = lambda w: pl.BlockSpec((rows, w), lambda b, i: (b * steps + i, 0))
    state = pl.BlockSpec((1,) + s0.shape[1:], lambda b, i: (b, 0, 0, 0))
    vec = pl.BlockSpec((1, LANES), lambda b, i: (0, 0))
    return pl.pallas_call(
        functools.partial(_delta_kernel, chunk=chunk, n_chunks=n_chunks, valid_rows=valid_rows),
        out_shape=[jax.ShapeDtypeStruct((t, n), F32), jax.ShapeDtypeStruct(s0.shape, F32)],
        grid=(n_seq, steps),
        in_specs=[blk(n), blk(n), blk(n), blk(LANES), blk(n), state, vec, vec, vec],
        out_specs=[blk(n), state],
        scratch_shapes=[pltpu.VMEM((stacked, DN_DIM), F32), pltpu.VMEM((stacked, DN_DIM), BF16),
                        pltpu.VMEM((stacked, DN_HEADS * chunk), BF16),
                        pltpu.VMEM((stacked, DN_DIM), BF16), pltpu.VMEM((stacked, DN_DIM), BF16),
                        pltpu.VMEM((n_chunks * SUBLANES, LANES), F32)],
        compiler_params=_params("arbitrary", "arbitrary"),
        name="delta",
    )(q, k, v, ab, z, s0, a_log, dt_bias, gain)


def _softplus2(x):
    return jnp.maximum(x, 0.0) + jnp.log(1.0 + jnp.exp2(jnp.minimum(x, -x))) * LOG2E


def _sb_weights(z2, later, usum, valid=None):
    n = usum.shape[0]
    sp = _softplus2(z2)
    if valid is not None:
        sp = jnp.where(valid, sp, 0.0)
    pieces = []
    for b in reversed(range(z2.shape[1] // n)):
        cols = slice(b * n, (b + 1) * n)
        within = _dot(sp[:, cols].astype(BF16), usum)
        pieces.append(jnp.exp2(z2[:, cols] - within - later))
        later = later + jnp.sum(sp[:, cols], axis=-1, keepdims=True)
    a = pieces[0] if len(pieces) == 1 else jnp.concatenate(pieces[::-1], axis=1)
    if valid is not None:
        a = jnp.where(valid, a, 0.0)
    return a.astype(BF16), later


def _suffix_sum_matrix(n):
    j = lax.broadcasted_iota(jnp.int32, (n, n), 0)
    s = lax.broadcasted_iota(jnp.int32, (n, n), 1)
    return (j >= s).astype(BF16)


def _sbp_kernel(bias_ref, q_ref, k_ref, v_ref, gain_ref, usum_ref, bd_ref, o_ref, *, tq):
    pair = pl.program_id(1)
    i = pl.program_id(2)
    lane = lax.broadcasted_iota(jnp.int32, (tq, LANES), 1)
    first = lane < SB_DIM
    q = q_ref[0] * (SB_DIM ** -0.5 * LOG2E)
    q_heads = (jnp.where(first, q, 0.0).astype(BF16), jnp.where(first, 0.0, q).astype(BF16))
    bias = (bias_ref[2 * pair] * LOG2E, bias_ref[2 * pair + 1] * LOG2E)
    usum = usum_ref[...]
    causal = lax.broadcasted_iota(jnp.int32, (tq, tq), 1) < lax.broadcasted_iota(jnp.int32, (tq, tq), 0)

    def block(j, carry, valid):
        rows = pl.ds(pl.multiple_of(j * tq, tq), tq)
        ks = k_ref[0, rows, :]
        vs = v_ref[0, rows, :]
        out = []
        for hh in range(2):
            later, acc = carry[hh]
            z2 = _dot_nt(q_heads[hh], ks) + bias[hh]
            a, later = _sb_weights(z2, later, usum, valid)
            out.append((later, acc + _dot(a, vs)))
        return tuple(out)

    zero = (jnp.zeros((tq, 1), F32), jnp.zeros((tq, LANES), F32))
    carry = block(i, (zero, zero), causal)
    carry = lax.fori_loop(0, i, lambda t, c: block(i - 1 - t, c, None), carry)
    o = jnp.where(first, carry[0][1], carry[1][1])
    ss = _dot(o * o, bd_ref[...], HIGHEST)
    o_ref[0] = o * lax.rsqrt(ss * (1.0 / SB_DIM) + EPS) * gain_ref[...]


def _sb_prompt(q, k16, v16, bias, gain2, n_seq, tq):
    t, n = q.shape
    l = t // n_seq
    pairs = n // LANES
    q3 = q.reshape(n_seq, l, n)
    k3 = k16.reshape(n_seq, l, n)
    v3 = v16.reshape(n_seq, l, n)
    seg = jnp.arange(LANES) // SB_DIM
    bd = (seg[:, None] == seg[None, :]).astype(F32)
    usum = _suffix_sum_matrix(SUFFIX_SUM_WIDTH)
    kv = pl.BlockSpec((1, l, LANES), lambda b, p, i: (b, 0, p))
    const = lambda b, p, i: (0, 0)
    out = pl.pallas_call(
        functools.partial(_sbp_kernel, tq=tq),
        out_shape=jax.ShapeDtypeStruct((n_seq, l, n), F32),
        grid=(n_seq, pairs, l // tq),
        in_specs=[pl.BlockSpec(memory_space=pltpu.SMEM),
                  pl.BlockSpec((1, tq, LANES), lambda b, p, i: (b, i, p)),
                  kv, kv,
                  pl.BlockSpec((1, LANES), const),
                  pl.BlockSpec(usum.shape, const),
                  pl.BlockSpec(bd.shape, const)],
        out_specs=pl.BlockSpec((1, tq, LANES), lambda b, p, i: (b, i, p)),
        compiler_params=_params("arbitrary", "arbitrary", "arbitrary"),
        name="sb_prompt",
    )(bias, q3, k3, v3, gain2, usum, bd)
    return out.reshape(t, n)


def _sbd_kernel(pt_ref, q_ref, bias_ref, knew_ref, vnew_ref, gain_ref, usum_ref, *rest, pages, n_new):
    k_pages = rest[:pages]
    v_pages = rest[pages:2 * pages]
    o_ref = rest[2 * pages]
    later_ref, acc_ref, kpad_ref, vpad_ref = rest[2 * pages + 1:]
    step = pl.program_id(1)
    n_rows, width = acc_ref.shape
    page = kpad_ref.shape[0]
    own_head = (lax.broadcasted_iota(jnp.int32, (n_rows, width), 1) // SB_DIM
                == lax.broadcasted_iota(jnp.int32, (n_rows, width), 0) % SB_HEADS)
    q = jnp.where(own_head, q_ref[0] * (SB_DIM ** -0.5 * LOG2E), 0.0).astype(BF16)
    bias = bias_ref[...] * LOG2E
    usum = usum_ref[...]

    def block(z2, weighted_sum, valid):
        a, later = _sb_weights(z2 + bias, later_ref[...], usum, valid)
        later_ref[...] = later
        acc_ref[...] += weighted_sum(a)

    @pl.when(step == 0)
    def _():
        later_ref[...] = jnp.zeros_like(later_ref)
        acc_ref[...] = jnp.zeros_like(acc_ref)
        kpad_ref[...] = jnp.zeros_like(kpad_ref)
        vpad_ref[...] = jnp.zeros_like(vpad_ref)
        kpad_ref[0:n_new, :] = knew_ref[0]
        vpad_ref[0:n_new, :] = vnew_ref[0]
        key = lax.broadcasted_iota(jnp.int32, (n_rows, page), 1)
        tok = lax.broadcasted_iota(jnp.int32, (n_rows, page), 0) // SB_HEADS
        vs = vpad_ref[...].astype(BF16)
        block(_dot_nt(q, kpad_ref[...].astype(BF16)), lambda a: _dot(a, vs), key < tok)

    for i in reversed(range(pages)):
        vt = v_pages[i][0, 0].astype(BF16)
        block(_dot(q, k_pages[i][0, 0].astype(BF16)), lambda a, vt=vt: _dot_nt(a, vt), None)

    @pl.when(step == pl.num_programs(1) - 1)
    def _():
        o = jnp.where(own_head, acc_ref[...], 0.0)
        ss = jnp.sum(o * o, axis=-1, keepdims=True)
        o = o * lax.rsqrt(ss * (1.0 / SB_DIM) + EPS)
        o = jnp.sum(o.reshape(n_rows // SB_HEADS, SB_HEADS, width), axis=1)
        o_ref[0] = o * gain_ref[...]


def _sb_decode(q, k_new, v_new, cache_k, cache_v, page_table, layer, bias, gain, pages):
    n_seq, n_new, n = q.shape
    page = cache_k.shape[3]
    n_pages = page_table.shape[1]
    steps = n_pages // pages
    n_rows = n_new * SB_HEADS
    q_rep = jnp.repeat(q, SB_HEADS, axis=1)
    bias_col = jnp.tile(bias, n_new).reshape(n_rows, 1)
    gain_row = jnp.tile(gain, SB_HEADS).reshape(1, n)
    usum = _suffix_sum_matrix(page)

    def page_spec(i):
        return pl.BlockSpec((1, 1, n, page),
                            lambda b, s, pt: (pt[b, (steps - 1 - s) * pages + i], layer, 0, 0))

    per_seq = lambda r: pl.BlockSpec((1, r, n), lambda b, s, pt: (b, 0, 0))
    const = lambda b, s, pt: (0, 0)
    grid_spec = pltpu.PrefetchScalarGridSpec(
        num_scalar_prefetch=1,
        grid=(n_seq, steps),
        in_specs=[per_seq(n_rows),
                  pl.BlockSpec(bias_col.shape, const),
                  per_seq(n_new), per_seq(n_new),
                  pl.BlockSpec(gain_row.shape, const),
                  pl.BlockSpec(usum.shape, const)]
                 + [page_spec(i) for i in range(pages)] * 2,
        out_specs=per_seq(n_new),
        scratch_shapes=[pltpu.VMEM((n_rows, 1), F32), pltpu.VMEM((n_rows, n), F32),
                        pltpu.VMEM((page, n), F32), pltpu.VMEM((page, n), F32)],
    )
    return pl.pallas_call(
        functools.partial(_sbd_kernel, pages=pages, n_new=n_new),
        out_shape=jax.ShapeDtypeStruct((n_seq, n_new, n), F32),
        grid_spec=grid_spec,
        compiler_params=_params("arbitrary", "arbitrary"),
        name="sb_decode",
    )(page_table, q_rep, bias_col, k_new, v_new, gain_row, usum,
      *([cache_k] * pages), *([cache_v] * pages))


def _outproj_kernel(a_ref, b_ref, h_ref, g_ref, wa_ref, wb_ref, o_ref):
    mix = _dot(a_ref[...].astype(BF16), wa_ref[...]) + _dot(b_ref[...].astype(BF16), wb_ref[...])
    o_ref[...] = h_ref[...] + _rms(mix, g_ref[3:4, :])


def _outproj(a, b, h, norms, w_a, w_b, tm):
    t, d = h.shape
    const = lambda i: (0, 0)
    row = lambda n: pl.BlockSpec((tm, n), lambda i: (i, 0))
    return pl.pallas_call(
        _outproj_kernel,
        out_shape=jax.ShapeDtypeStruct((t, d), F32),
        grid=(t // tm,),
        in_specs=[row(a.shape[1]), row(b.shape[1]), row(d), pl.BlockSpec(norms.shape, const),
                  pl.BlockSpec(w_a.shape, const), pl.BlockSpec(w_b.shape, const)],
        out_specs=row(d),
        compiler_params=_params("arbitrary"),
        name="outproj",
    )(a, b, h, norms, w_a, w_b)


def _pad_lanes(v):
    return jnp.pad(v.astype(F32), (0, LANES - v.shape[0])).reshape(1, LANES)


def _run_group(x, weights, *, tm, past):
    n_seq, l, d = x.shape
    depth = weights["norms"].shape[0]
    nq = DN_HEADS * DN_DIM
    sb = SB_HEADS * SB_DIM
    h = x.reshape(n_seq * l, d)
    ks, vs, ss, cs = [], [], [], []
    for layer in range(depth):
        norms = weights["norms"][layer]
        w_in = weights["w_in"][layer].astype(BF16)
        w_qkv = w_in[:, :3 * nq]
        w_z = w_in[:, 3 * nq:4 * nq]
        w_ab = jnp.pad(w_in[:, 4 * nq:4 * nq + 2 * DN_HEADS], ((0, 0), (0, LANES - 2 * DN_HEADS)))
        w_sb = w_in[:, 4 * nq + 2 * DN_HEADS:]
        w_o = weights["w_o"][layer].astype(BF16)
        a_log = _pad_lanes(weights["a_log"][layer])
        dt_bias = _pad_lanes(weights["dt_bias"][layer])
        dn_gain = weights["dn_out_norm"][layer].reshape(1, DN_DIM)
        sb_bias = weights["sb_logit_bias"][layer]
        sb_gain = weights["sb_out_norm"][layer]

        h = _ffn(h, norms, weights["ffn1_w_gate_up"][layer].astype(BF16),
                 weights["ffn1_w_down"][layer].astype(BF16), 0, 1, tm)
        qkv, z, ab, q_sb, k_sb, v_sb, k16, v16 = _inproj(h, norms, w_qkv, w_z, w_ab, w_sb, tm)
        conv_w = weights["conv_w"][layer]
        if past is None:
            hist8 = jnp.zeros((n_seq, SUBLANES, 3 * nq), F32)
            q_dn, k_dn, v_dn = _conv(qkv, hist8, conv_w, n_seq, tm)
            s0 = jnp.zeros((n_seq, DN_HEADS, DN_DIM, DN_DIM), F32)
            o_dn, s_new = _delta(q_dn, k_dn, v_dn, ab, z, s0, a_log, dt_bias, dn_gain,
                                 n_seq, chunk=64, n_chunks=8, valid_rows=64)
            o_sb = _sb_prompt(q_sb, k16, v16, sb_bias, jnp.tile(sb_gain, 2).reshape(1, LANES), n_seq, tq=512)
            conv_new = qkv.reshape(n_seq, l, 3 * nq)[:, l - (CONV_W - 1):]
        else:
            conv_state, delta_state, cache_k, cache_v, page_table = past
            qkv3 = qkv.reshape(n_seq, l, 3 * nq)
            hist = conv_state[layer]
            tile = jnp.concatenate(
                [hist, qkv3, jnp.zeros((n_seq, SUBLANES - l - hist.shape[1], 3 * nq), F32)], axis=1)
            tile = tile.reshape(n_seq * SUBLANES, 3 * nq)
            conv_out = _conv(tile, jnp.zeros((n_seq, SUBLANES, 3 * nq), F32), conv_w, n_seq, SUBLANES)
            first = hist.shape[1]

            def one_chunk(a):
                return jnp.pad(a, ((0, 0), (0, SAMPLE_CHUNK - l), (0, 0))).reshape(n_seq * SAMPLE_CHUNK, -1)

            q_dn, k_dn, v_dn = (one_chunk(a.reshape(n_seq, SUBLANES, nq)[:, first:first + l]) for a in conv_out)
            o_dn, s_new = _delta(q_dn, k_dn, v_dn, one_chunk(ab.reshape(n_seq, l, -1)),
                                 one_chunk(z.reshape(n_seq, l, -1)), delta_state[layer], a_log, dt_bias,
                                 dn_gain, n_seq, chunk=SAMPLE_CHUNK, n_chunks=1, valid_rows=l)
            o_dn = o_dn.reshape(n_seq, SAMPLE_CHUNK, nq)[:, :l].reshape(n_seq * l, nq)
            o_sb = _sb_decode(q_sb.reshape(n_seq, l, sb), k_sb.reshape(n_seq, l, sb), v_sb.reshape(n_seq, l, sb),
                              cache_k, cache_v, page_table, layer, sb_bias, sb_gain, pages=8)
            o_sb = o_sb.reshape(n_seq * l, sb)
            conv_new = jnp.concatenate([hist, qkv3], axis=1)[:, -(CONV_W - 1):]
        h = _outproj(o_dn, o_sb, h, norms, w_o[:nq], w_o[nq:], tm)
        h = _ffn(h, norms, weights["ffn2_w_gate_up"][layer].astype(BF16),
                 weights["ffn2_w_down"][layer].astype(BF16), 4, 5, tm)
        ks.append(k_sb.reshape(n_seq, l, SB_HEADS, SB_DIM))
        vs.append(v_sb.reshape(n_seq, l, SB_HEADS, SB_DIM))
        ss.append(s_new)
        cs.append(conv_new)
    return (h.reshape(n_seq, l, d), jnp.stack(ks, axis=2), jnp.stack(vs, axis=2),
            jnp.stack(ss, axis=0), jnp.stack(cs, axis=0))


def kernel(x_prompt, x_sample, cache_sb_k, cache_sb_v, page_table, state_delta, state_conv, norms,
           ffn1_w_gate_up, ffn1_w_down, w_in, conv_w, a_log, dt_bias, dn_out_norm, sb_logit_bias, sb_out_norm,
           w_o, ffn2_w_gate_up, ffn2_w_down):
    weights = dict(norms=norms, ffn1_w_gate_up=ffn1_w_gate_up, ffn1_w_down=ffn1_w_down, w_in=w_in,
                   conv_w=conv_w, a_log=a_log, dt_bias=dt_bias, dn_out_norm=dn_out_norm,
                   sb_logit_bias=sb_logit_bias, sb_out_norm=sb_out_norm, w_o=w_o,
                   ffn2_w_gate_up=ffn2_w_gate_up, ffn2_w_down=ffn2_w_down)
    n_phys, page, depth = cache_sb_k.shape[:3]
    cache_k = jnp.transpose(cache_sb_k, (0, 2, 3, 4, 1)).reshape(n_phys, depth, -1, page)
    cache_v = jnp.transpose(cache_sb_v, (0, 2, 3, 4, 1)).reshape(n_phys, depth, -1, page)
    y_p, k_p, v_p, d_p, c_p = _run_group(x_prompt, weights, tm=512, past=None)
    n_tok = x_sample.shape[0] * x_sample.shape[1]
    y_s, k_s, v_s, d_s, c_s = _run_group(x_sample, weights, tm=n_tok,
                                         past=(state_conv, state_delta, cache_k, cache_v, page_table))
    return (y_p, y_s, k_p, v_p, k_s, v_s, d_p, d_s, c_p, c_s)
```

```python
import functools

import jax
import jax.numpy as jnp
from jax import lax
from jax.experimental import pallas as pl
from jax.experimental.pallas import tpu as pltpu

F32 = jnp.float32
BF16 = jnp.bfloat16
HIGHEST = lax.Precision.HIGHEST

EPS = 1e-6
DN_HEADS = 4
DN_DIM = 128
SB_HEADS = 8
SB_DIM = 64
CONV_W = 4
LANES = 128
SUBLANES = 8
VMEM_LIMIT = 56 * 1024 * 1024
MXU_DIM = 256
SUFFIX_SUM_WIDTH = MXU_DIM
LOG2E = 1.4426950408889634
SAMPLE_CHUNK = 16


def _dot(a, b, precision=None):
    return jnp.dot(a, b, preferred_element_type=F32, precision=precision)


def _dot_nt(a, b):
    return lax.dot_general(a, b, (((1,), (1,)), ((), ())), preferred_element_type=F32)


def _dot_tn(a, b):
    return lax.dot_general(a, b, (((0,), (0,)), ((), ())), preferred_element_type=F32)


def _rms(x, gain):
    ms = jnp.mean(x * x, axis=-1, keepdims=True)
    return x * lax.rsqrt(ms + EPS) * gain


def _sigmoid(x):
    return 1.0 / (1.0 + jnp.exp(-x))


def _softplus(x):
    return jnp.maximum(x, 0.0) + jnp.log(1.0 + jnp.exp(-jnp.abs(x)))


def _params(*sem):
    return pltpu.CompilerParams(dimension_semantics=sem, vmem_limit_bytes=VMEM_LIMIT)


def _ffn_kernel(x_ref, g_ref, wgu_ref, wd_ref, o_ref, acc_ref, *, pre, post, d_ff, chunk):
    x = x_ref[...]
    xn = _rms(x, g_ref[pre:pre + 1, :]).astype(BF16)
    for c in range(d_ff // chunk):
        gate = _dot(xn, wgu_ref[:, c * chunk:(c + 1) * chunk])
        up = _dot(xn, wgu_ref[:, d_ff + c * chunk:d_ff + (c + 1) * chunk])
        h = (gate * _sigmoid(gate) * up).astype(BF16)
        part = _dot(h, wd_ref[c * chunk:(c + 1) * chunk, :])
        if c == 0:
            acc_ref[...] = part
        else:
            acc_ref[...] += part
    o_ref[...] = x + 0.5 * _rms(acc_ref[...], g_ref[post:post + 1, :])


def _ffn(x, norms, w_gu, w_d, pre, post, tm):
    t, d = x.shape
    d_ff = w_d.shape[0]
    const = lambda i: (0, 0)
    return pl.pallas_call(
        functools.partial(_ffn_kernel, pre=pre, post=post, d_ff=d_ff, chunk=256),
        out_shape=jax.ShapeDtypeStruct((t, d), F32),
        grid=(t // tm,),
        in_specs=[pl.BlockSpec((tm, d), lambda i: (i, 0)),
                  pl.BlockSpec(norms.shape, const),
                  pl.BlockSpec(w_gu.shape, const),
                  pl.BlockSpec(w_d.shape, const)],
        out_specs=pl.BlockSpec((tm, d), lambda i: (i, 0)),
        scratch_shapes=[pltpu.VMEM((tm, d), F32)],
        compiler_params=_params("arbitrary"),
        name="ffn",
    )(x, norms, w_gu, w_d)


def _inproj_kernel(*refs, sb, n_prev, feature_major):
    h_ref, g_ref, wqkv_ref, wz_ref, wab_ref, wsb_ref = refs[:6]
    prev = refs[6:8] if n_prev else ()
    qkv_ref, z_ref, ab_ref, q_ref, kb_ref, vb_ref, k_ref, v_ref = refs[6 + len(prev):]
    u = _rms(h_ref[...], g_ref[2:3, :]).astype(BF16)
    qkv_ref[...] = _dot(u, wqkv_ref[...])
    z_ref[...] = _dot(u, wz_ref[...])
    ab_ref[...] = _dot(u, wab_ref[...])
    p = _dot(u, wsb_ref[...])
    k = p[:, sb:2 * sb]
    v = p[:, 2 * sb:]
    q_ref[...] = p[:, :sb]
    kb_ref[...] = k.astype(BF16)
    vb_ref[...] = v.astype(BF16)
    if feature_major:
        for new, old in zip((k_ref, v_ref), prev):
            new[0, 0:n_prev] = old[0]
        k_ref[0, n_prev] = k.T
        v_ref[0, n_prev] = v.T
    else:
        k_ref[...] = k
        v_ref[...] = v


def _inproj(h, norms, w_qkv, w_z, w_ab, w_sb, n_seq, tm, prev_rows):
    t, d = h.shape
    sb = SB_HEADS * SB_DIM
    nt = t // n_seq // tm
    const = lambda b, i: (0, 0)
    row = lambda n: pl.BlockSpec((tm, n), lambda b, i: (b * nt + i, 0))
    widths = (w_qkv.shape[1], w_z.shape[1], LANES, sb, sb, sb)
    dtypes = (F32, F32, F32, F32, BF16, BF16)
    out_shape = [jax.ShapeDtypeStruct((t, n), dt) for n, dt in zip(widths, dtypes)]
    out_specs = [row(n) for n in widths]
    prev = ()
    n_prev = 0
    if prev_rows is None:
        out_shape += [jax.ShapeDtypeStruct((t, sb), F32)] * 2
        out_specs += [row(sb)] * 2
    else:
        if prev_rows[0] is not None:
            prev = tuple(prev_rows)
            n_prev = prev[0].shape[1]
        layers = lambda n: pl.BlockSpec((1, n, sb, tm), lambda b, i: (b, 0, 0, i))
        out_shape += [jax.ShapeDtypeStruct((n_seq, n_prev + 1, sb, t // n_seq), F32)] * 2
        out_specs += [layers(n_prev + 1)] * 2
    return pl.pallas_call(
        functools.partial(_inproj_kernel, sb=sb, n_prev=n_prev, feature_major=prev_rows is not None),
        out_shape=out_shape,
        grid=(n_seq, nt),
        in_specs=[row(d), pl.BlockSpec(norms.shape, const)]
                 + [pl.BlockSpec(w.shape, const) for w in (w_qkv, w_z, w_ab, w_sb)]
                 + [layers(n_prev) for _ in prev],
        out_specs=out_specs,
        compiler_params=_params("arbitrary", "arbitrary"),
        name="inproj",
    )(h, norms, w_qkv, w_z, w_ab, w_sb, *prev)


def _conv_kernel(x_ref, prev_ref, hist_ref, w_ref, q_ref, k_ref, v_ref):
    x = x_ref[...]
    w = w_ref[...]
    prev = jnp.where(pl.program_id(1) == 0, hist_ref[0], prev_ref[...])
    row = lax.broadcasted_iota(jnp.int32, prev.shape, 0)
    y = x * w[CONV_W - 1:CONV_W, :]
    y_head = y[0:SUBLANES]
    for s in range(1, CONV_W):
        tap = w[CONV_W - 1 - s:CONV_W - s, :]
        xs = pltpu.roll(x, s, 0)
        y = y + xs * tap
        head = jnp.where(row < s, pltpu.roll(prev, s, 0), xs[0:SUBLANES])
        y_head = y_head + head * tap

    def finish(y, rows):
        y = y * _sigmoid(y)
        nq = DN_HEADS * DN_DIM
        for h in range(DN_HEADS):
            for ref, off, scale in ((q_ref, 0, DN_DIM ** -0.5), (k_ref, nq, 1.0)):
                t = y[:, off + h * DN_DIM:off + (h + 1) * DN_DIM]
                ss = jnp.sum(t * t, axis=-1, keepdims=True)
                ref[rows, h * DN_DIM:(h + 1) * DN_DIM] = t * (lax.rsqrt(ss + EPS) * scale)
        v_ref[rows, :] = y[:, 2 * nq:]

    finish(y, slice(None))
    finish(y_head, slice(0, SUBLANES))


def _conv(x, hist8, w, n_seq, tm):
    t, c = x.shape
    nt = t // n_seq // tm
    blocks8 = tm // SUBLANES
    n = DN_HEADS * DN_DIM
    out = pl.BlockSpec((tm, n), lambda b, i: (b * nt + i, 0))
    return pl.pallas_call(
        _conv_kernel,
        out_shape=[jax.ShapeDtypeStruct((t, n), F32)] * 3,
        grid=(n_seq, nt),
        in_specs=[pl.BlockSpec((tm, c), lambda b, i: (b * nt + i, 0)),
                  pl.BlockSpec((SUBLANES, c), lambda b, i: (jnp.maximum((b * nt + i) * blocks8 - 1, 0), 0)),
                  pl.BlockSpec((1, SUBLANES, c), lambda b, i: (b, 0, 0)),
                  pl.BlockSpec(w.shape, lambda b, i: (0, 0))],
        out_specs=[out, out, out],
        compiler_params=_params("arbitrary", "arbitrary"),
        name="conv",
    )(x, x, hist8, w)


def _split(x, terms):
    parts = []
    for _ in range(terms):
        p = x.astype(BF16)
        parts.append(p)
        x = x - p.astype(F32)
    return parts


def _dot_left01(a01, b):
    return sum(_dot(a01, p) for p in _split(b, 3))


def _dot_split(a, b):
    a_hi, a_lo = _split(a, 2)
    b_hi, b_lo = _split(b, 2)
    return _dot(a_hi, b_hi) + _dot(a_lo, b_hi) + _dot(a_hi, b_lo)


def _delta_kernel(q_ref, k_ref, v_ref, ab_ref, z_ref, s0_ref, alog_ref, dtb_ref, gain_ref,
                  o_ref, s_ref, u_ref, w_ref, qk_ref, qe_ref, kd_ref, decay_last_ref,
                  *, chunk, n_chunks, valid_rows):
    c = chunk

    @pl.when(pl.program_id(1) == 0)
    def _():
        s_ref[...] = s0_ref[...]

    r = DN_HEADS * c
    ri = lax.broadcasted_iota(jnp.int32, (r, r), 0)
    ci = lax.broadcasted_iota(jnp.int32, (r, r), 1)
    same_head = (ri // c) == (ci // c)
    incl = same_head & (ri >= ci)
    strict = same_head & (ri > ci)
    eye = ri == ci
    ones = jnp.ones((r, r), BF16)
    ltri = (lax.broadcasted_iota(jnp.int32, (c, c), 0) >= lax.broadcasted_iota(jnp.int32, (c, c), 1)).astype(BF16)
    neg_a = -jnp.exp(alog_ref[...])
    dtb = dtb_ref[...]
    gain = gain_ref[...]
    n_doublings = max(c.bit_length() - 1, 0)
    heads = range(DN_HEADS)

    def head_cols(h):
        return slice(h * DN_DIM, (h + 1) * DN_DIM)

    def stacked_rows(ch):
        return pl.ds(pl.multiple_of(ch * r, r), r)

    def local(ch):
        rows = pl.ds(pl.multiple_of(ch * c, c), c)
        gb = ab_ref[rows, :]
        g_all = neg_a * _softplus(gb + dtb)
        beta_all = _sigmoid(gb)
        if valid_rows < c:
            live = lax.broadcasted_iota(jnp.int32, g_all.shape, 0) < valid_rows
            g_all = jnp.where(live, g_all, 0.0)
            beta_all = jnp.where(live, beta_all, 0.0)
        gcum_all = _dot_left01(ltri, g_all)
        decay_last_ref[pl.ds(pl.multiple_of(ch * SUBLANES, SUBLANES), SUBLANES), :] = jnp.broadcast_to(
            jnp.exp(gcum_all[c - 1:c, :]), (SUBLANES, LANES))
        beta = jnp.concatenate([beta_all[:, DN_HEADS + h:DN_HEADS + h + 1] for h in heads], axis=0)
        gcum = jnp.concatenate([gcum_all[:, h:h + 1] for h in heads], axis=0)
        g_last = jnp.concatenate([jnp.broadcast_to(gcum_all[c - 1:c, h:h + 1], (c, 1)) for h in heads], axis=0)
        gcum_cols = jnp.broadcast_to(gcum, (r, r))
        gcum_rows = _dot_left01(ones, jnp.where(eye, gcum_cols, 0.0))
        decay = jnp.where(incl, jnp.exp(jnp.minimum(gcum_cols - gcum_rows, 0.0)), 0.0)
        q = jnp.concatenate([q_ref[rows, head_cols(h)] for h in heads], axis=0)
        k = jnp.concatenate([k_ref[rows, head_cols(h)] for h in heads], axis=0)
        v = jnp.concatenate([v_ref[rows, head_cols(h)] for h in heads], axis=0)
        kb = k * beta
        k16 = k.astype(BF16)
        m = jnp.where(strict, _dot_nt(kb.astype(BF16), k16) * decay, 0.0)
        out = stacked_rows(ch)
        qk_ref[out, :] = (_dot_nt(q.astype(BF16), k16) * decay).astype(BF16)
        inv = jnp.where(eye, 1.0, 0.0) - m
        pw = _dot_split(m, m)
        for step in range(1, n_doublings):
            inv = inv + _dot_split(inv, pw)
            if step + 1 < n_doublings:
                pw = _dot_split(pw, pw)
        e_gcum = jnp.exp(gcum)
        sol = _dot_split(inv, jnp.concatenate([v * beta, kb * e_gcum], axis=1))
        u_ref[out, :] = sol[:, :DN_DIM]
        w_ref[out, :] = sol[:, DN_DIM:].astype(BF16)
        qe_ref[out, :] = (q * e_gcum).astype(BF16)
        kd_ref[out, :] = (k * jnp.exp(g_last - gcum)).astype(BF16)

    if n_chunks % 2 == 0:
        def local_pair(i, carry):
            local(2 * i)
            local(2 * i + 1)
            return carry
        lax.fori_loop(0, n_chunks // 2, local_pair, 0)
    else:
        for ch in range(n_chunks):
            local(ch)

    def scan(ch, carry):
        rows = pl.ds(pl.multiple_of(ch * c, c), c)
        decay_last = decay_last_ref[pl.ds(pl.multiple_of(ch * SUBLANES, SUBLANES), SUBLANES), :]
        s = [s_ref[0, h] for h in heads]
        s16 = [x.astype(BF16) for x in s]
        head_rows = [pl.ds(pl.multiple_of(ch * r + h * c, c), c) for h in heads]
        u16 = [(u_ref[head_rows[h], :] - _dot(w_ref[head_rows[h], :], s16[h])).astype(BF16) for h in heads]
        u16_all = jnp.concatenate(u16, axis=0)
        for h in heads:
            o = _dot(qe_ref[head_rows[h], :], s16[h]) + _dot(qk_ref[head_rows[h], :], u16_all)
            s_ref[0, h] = s[h] * decay_last[0:1, h:h + 1] + _dot_tn(kd_ref[head_rows[h], :], u16[h])
            zz = z_ref[rows, head_cols(h)]
            o_ref[rows, head_cols(h)] = _rms(o, gain) * (zz * _sigmoid(zz))
        return carry

    lax.fori_loop(0, n_chunks, scan, 0)


def _delta(q, k, v, ab, z, s0, a_log, dt_bias, gain, n_seq, chunk, n_chunks, valid_rows):
    t, n = q.shape
    rows = chunk * n_chunks
    stacked = DN_HEADS * rows
    steps = t // n_seq // rows
    blk = lambda w: pl.BlockSpec((rows, w), lambda b, i: (b * steps + i, 0))
    state = pl.BlockSpec((1,) + s0.shape[1:], lambda b, i: (b, 0, 0, 0))
    vec = pl.BlockSpec((1, LANES), lambda b, i: (0, 0))
    return pl.pallas_call(
        functools.partial(_delta_kernel, chunk=chunk, n_chunks=n_chunks, valid_rows=valid_rows),
        out_shape=[jax.ShapeDtypeStruct((t, n), F32), jax.ShapeDtypeStruct(s0.shape, F32)],
        grid=(n_seq, steps),
        in_specs=[blk(n), blk(n), blk(n), blk(LANES), blk(n), state, vec, vec, vec],
        out_specs=[blk(n), state],
        scratch_shapes=[pltpu.VMEM((stacked, DN_DIM), F32), pltpu.VMEM((stacked, DN_DIM), BF16),
                        pltpu.VMEM((stacked, DN_HEADS * chunk), BF16),
                        pltpu.VMEM((stacked, DN_DIM), BF16), pltpu.VMEM((stacked, DN_DIM), BF16),
                        pltpu.VMEM((n_chunks * SUBLANES, LANES), F32)],
        compiler_params=_params("arbitrary", "arbitrary"),
        name="delta",
    )(q, k, v, ab, z, s0, a_log, dt_bias, gain)


def _softplus2(x):
    return jnp.maximum(x, 0.0) + jnp.log(1.0 + jnp.exp2(jnp.minimum(x, -x))) * LOG2E


def _sb_weights(z2, later, usum, valid=None):
    n = usum.shape[0]
    sp = _softplus2(z2)
    if valid is not None:
        sp = jnp.where(valid, sp, 0.0)
    pieces = []
    for b in reversed(range(z2.shape[1] // n)):
        cols = slice(b * n, (b + 1) * n)
        within = _dot(sp[:, cols].astype(BF16), usum)
        pieces.append(jnp.exp2(z2[:, cols] - within - later))
        later = later + jnp.sum(sp[:, cols], axis=-1, keepdims=True)
    a = pieces[0] if len(pieces) == 1 else jnp.concatenate(pieces[::-1], axis=1)
    if valid is not None:
        a = jnp.where(valid, a, 0.0)
    return a.astype(BF16), later


def _suffix_sum_matrix(n):
    j = lax.broadcasted_iota(jnp.int32, (n, n), 0)
    s = lax.broadcasted_iota(jnp.int32, (n, n), 1)
    return (j >= s).astype(BF16)


def _sbp_kernel(bias_ref, q_ref, k_ref, v_ref, gain_ref, usum_ref, bd_ref, o_ref, *, tq):
    pair = pl.program_id(1)
    i = pl.program_id(2)
    lane = lax.broadcasted_iota(jnp.int32, (tq, LANES), 1)
    first = lane < SB_DIM
    q = q_ref[0] * (SB_DIM ** -0.5 * LOG2E)
    q_heads = (jnp.where(first, q, 0.0).astype(BF16), jnp.where(first, 0.0, q).astype(BF16))
    bias = (bias_ref[2 * pair] * LOG2E, bias_ref[2 * pair + 1] * LOG2E)
    usum = usum_ref[...]
    causal = lax.broadcasted_iota(jnp.int32, (tq, tq), 1) < lax.broadcasted_iota(jnp.int32, (tq, tq), 0)

    def block(j, carry, valid):
        rows = pl.ds(pl.multiple_of(j * tq, tq), tq)
        ks = k_ref[0, rows, :]
        vs = v_ref[0, rows, :]
        out = []
        for hh in range(2):
            later, acc = carry[hh]
            z2 = _dot_nt(q_heads[hh], ks) + bias[hh]
            a, later = _sb_weights(z2, later, usum, valid)
            out.append((later, acc + _dot(a, vs)))
        return tuple(out)

    zero = (jnp.zeros((tq, 1), F32), jnp.zeros((tq, LANES), F32))
    carry = block(i, (zero, zero), causal)
    carry = lax.fori_loop(0, i, lambda t, c: block(i - 1 - t, c, None), carry)
    o = jnp.where(first, carry[0][1], carry[1][1])
    ss = _dot(o * o, bd_ref[...], HIGHEST)
    o_ref[0] = o * lax.rsqrt(ss * (1.0 / SB_DIM) + EPS) * gain_ref[...]


def _sb_prompt(q, k16, v16, bias, gain2, n_seq, tq):
    t, n = q.shape
    l = t // n_seq
    pairs = n // LANES
    q3 = q.reshape(n_seq, l, n)
    k3 = k16.reshape(n_seq, l, n)
    v3 = v16.reshape(n_seq, l, n)
    seg = jnp.arange(LANES) // SB_DIM
    bd = (seg[:, None] == seg[None, :]).astype(F32)
    usum = _suffix_sum_matrix(SUFFIX_SUM_WIDTH)
    kv = pl.BlockSpec((1, l, LANES), lambda b, p, i: (b, 0, p))
    const = lambda b, p, i: (0, 0)
    out = pl.pallas_call(
        functools.partial(_sbp_kernel, tq=tq),
        out_shape=jax.ShapeDtypeStruct((n_seq, l, n), F32),
        grid=(n_seq, pairs, l // tq),
        in_specs=[pl.BlockSpec(memory_space=pltpu.SMEM),
                  pl.BlockSpec((1, tq, LANES), lambda b, p, i: (b, i, p)),
                  kv, kv,
                  pl.BlockSpec((1, LANES), const),
                  pl.BlockSpec(usum.shape, const),
                  pl.BlockSpec(bd.shape, const)],
        out_specs=pl.BlockSpec((1, tq, LANES), lambda b, p, i: (b, i, p)),
        compiler_params=_params("arbitrary", "arbitrary", "arbitrary"),
        name="sb_prompt",
    )(bias, q3, k3, v3, gain2, usum, bd)
    return out.reshape(t, n)


def _sbd_kernel(pt_ref, q_ref, bias_ref, knew_ref, vnew_ref, gain_ref, usum_ref, *rest, pages, n_new):
    k_pages = rest[:pages]
    v_pages = rest[pages:2 * pages]
    o_ref = rest[2 * pages]
    later_ref, acc_ref, kpad_ref, vpad_ref = rest[2 * pages + 1:]
    step = pl.program_id(1)
    n_rows, width = acc_ref.shape
    page = kpad_ref.shape[0]
    own_head = (lax.broadcasted_iota(jnp.int32, (n_rows, width), 1) // SB_DIM
                == lax.broadcasted_iota(jnp.int32, (n_rows, width), 0) % SB_HEADS)
    q = jnp.where(own_head, q_ref[0] * (SB_DIM ** -0.5 * LOG2E), 0.0).astype(BF16)
    bias = bias_ref[...] * LOG2E
    usum = usum_ref[...]

    def block(z2, weighted_sum, valid):
        a, later = _sb_weights(z2 + bias, later_ref[...], usum, valid)
        later_ref[...] = later
        acc_ref[...] += weighted_sum(a)

    @pl.when(step == 0)
    def _():
        later_ref[...] = jnp.zeros_like(later_ref)
        acc_ref[...] = jnp.zeros_like(acc_ref)
        kpad_ref[...] = jnp.zeros_like(kpad_ref)
        vpad_ref[...] = jnp.zeros_like(vpad_ref)
        kpad_ref[0:n_new, :] = knew_ref[0]
        vpad_ref[0:n_new, :] = vnew_ref[0]
        key = lax.broadcasted_iota(jnp.int32, (n_rows, page), 1)
        tok = lax.broadcasted_iota(jnp.int32, (n_rows, page), 0) // SB_HEADS
        vs = vpad_ref[...].astype(BF16)
        block(_dot_nt(q, kpad_ref[...].astype(BF16)), lambda a: _dot(a, vs), key < tok)

    def weighted_sum(a):
        return sum(_dot_nt(a[:, i * page:(i + 1) * page], v_pages[i][0, 0].astype(BF16)) for i in range(pages))

    block(jnp.concatenate([_dot(q, k_pages[i][0, 0].astype(BF16)) for i in range(pages)], axis=1),
          weighted_sum, None)

    @pl.when(step == pl.num_programs(1) - 1)
    def _():
        o = jnp.where(own_head, acc_ref[...], 0.0)
        ss = jnp.sum(o * o, axis=-1, keepdims=True)
        o = o * lax.rsqrt(ss * (1.0 / SB_DIM) + EPS)
        o = jnp.sum(o.reshape(n_rows // SB_HEADS, SB_HEADS, width), axis=1)
        o_ref[0] = o * gain_ref[...]


def _sb_decode(q, k_new, v_new, cache_k, cache_v, page_table, layer, bias, gain, pages):
    n_seq, n_new, n = q.shape
    page = cache_k.shape[3]
    n_pages = page_table.shape[1]
    steps = n_pages // pages
    n_rows = n_new * SB_HEADS
    q_rep = jnp.repeat(q, SB_HEADS, axis=1)
    bias_col = jnp.tile(bias, n_new).reshape(n_rows, 1)
    gain_row = jnp.tile(gain, SB_HEADS).reshape(1, n)
    usum = _suffix_sum_matrix(page)

    def page_spec(i):
        return pl.BlockSpec((1, 1, n, page),
                            lambda b, s, pt: (pt[b, (steps - 1 - s) * pages + i], layer, 0, 0))

    per_seq = lambda r: pl.BlockSpec((1, r, n), lambda b, s, pt: (b, 0, 0))
    const = lambda b, s, pt: (0, 0)
    grid_spec = pltpu.PrefetchScalarGridSpec(
        num_scalar_prefetch=1,
        grid=(n_seq, steps),
        in_specs=[per_seq(n_rows),
                  pl.BlockSpec(bias_col.shape, const),
                  per_seq(n_new), per_seq(n_new),
                  pl.BlockSpec(gain_row.shape, const),
                  pl.BlockSpec(usum.shape, const)]
                 + [page_spec(i) for i in range(pages)] * 2,
        out_specs=per_seq(n_new),
        scratch_shapes=[pltpu.VMEM((n_rows, 1), F32), pltpu.VMEM((n_rows, n), F32),
                        pltpu.VMEM((page, n), F32), pltpu.VMEM((page, n), F32)],
    )
    return pl.pallas_call(
        functools.partial(_sbd_kernel, pages=pages, n_new=n_new),
        out_shape=jax.ShapeDtypeStruct((n_seq, n_new, n), F32),
        grid_spec=grid_spec,
        compiler_params=_params("arbitrary", "arbitrary"),
        name="sb_decode",
    )(page_table, q_rep, bias_col, k_new, v_new, gain_row, usum,
      *([cache_k] * pages), *([cache_v] * pages))


def _outproj_kernel(a_ref, b_ref, h_ref, g_ref, wa_ref, wb_ref, o_ref):
    mix = _dot(a_ref[...].astype(BF16), wa_ref[...]) + _dot(b_ref[...].astype(BF16), wb_ref[...])
    o_ref[...] = h_ref[...] + _rms(mix, g_ref[3:4, :])


def _outproj(a, b, h, norms, w_a, w_b, tm):
    t, d = h.shape
    const = lambda i: (0, 0)
    row = lambda n: pl.BlockSpec((tm, n), lambda i: (i, 0))
    return pl.pallas_call(
        _outproj_kernel,
        out_shape=jax.ShapeDtypeStruct((t, d), F32),
        grid=(t // tm,),
        in_specs=[row(a.shape[1]), row(b.shape[1]), row(d), pl.BlockSpec(norms.shape, const),
                  pl.BlockSpec(w_a.shape, const), pl.BlockSpec(w_b.shape, const)],
        out_specs=row(d),
        compiler_params=_params("arbitrary"),
        name="outproj",
    )(a, b, h, norms, w_a, w_b)


def _pad_lanes(v):
    return jnp.pad(v.astype(F32), (0, LANES - v.shape[0])).reshape(1, LANES)


def _run_group(x, weights, *, tm, past):
    n_seq, l, d = x.shape
    depth = weights["norms"].shape[0]
    nq = DN_HEADS * DN_DIM
    sb = SB_HEADS * SB_DIM
    h = x.reshape(n_seq * l, d)
    ks, vs, ss, cs = [], [], [], []
    kv_rows = (None, None)
    for layer in range(depth):
        norms = weights["norms"][layer]
        w_in = weights["w_in"][layer].astype(BF16)
        w_qkv = w_in[:, :3 * nq]
        w_z = w_in[:, 3 * nq:4 * nq]
        w_ab = jnp.pad(w_in[:, 4 * nq:4 * nq + 2 * DN_HEADS], ((0, 0), (0, LANES - 2 * DN_HEADS)))
        w_sb = w_in[:, 4 * nq + 2 * DN_HEADS:]
        w_o = weights["w_o"][layer].astype(BF16)
        a_log = _pad_lanes(weights["a_log"][layer])
        dt_bias = _pad_lanes(weights["dt_bias"][layer])
        dn_gain = weights["dn_out_norm"][layer].reshape(1, DN_DIM)
        sb_bias = weights["sb_logit_bias"][layer]
        sb_gain = weights["sb_out_norm"][layer]

        h = _ffn(h, norms, weights["ffn1_w_gate_up"][layer].astype(BF16),
                 weights["ffn1_w_down"][layer].astype(BF16), 0, 1, tm)
        qkv, z, ab, q_sb, k16, v16, k_sb, v_sb = _inproj(
            h, norms, w_qkv, w_z, w_ab, w_sb, 1 if past else n_seq, tm, None if past else kv_rows)
        conv_w = weights["conv_w"][layer]
        if past is None:
            kv_rows = (k_sb, v_sb)
            hist8 = jnp.zeros((n_seq, SUBLANES, 3 * nq), F32)
            q_dn, k_dn, v_dn = _conv(qkv, hist8, conv_w, n_seq, tm)
            s0 = jnp.zeros((n_seq, DN_HEADS, DN_DIM, DN_DIM), F32)
            o_dn, s_new = _delta(q_dn, k_dn, v_dn, ab, z, s0, a_log, dt_bias, dn_gain,
                                 n_seq, chunk=64, n_chunks=8, valid_rows=64)
            o_sb = _sb_prompt(q_sb, k16, v16, sb_bias, jnp.tile(sb_gain, 2).reshape(1, LANES), n_seq, tq=512)
            conv_new = qkv.reshape(n_seq, l, 3 * nq)[:, l - (CONV_W - 1):]
        else:
            conv_state, delta_state, cache_k, cache_v, page_table = past
            qkv3 = qkv.reshape(n_seq, l, 3 * nq)
            hist = conv_state[layer]
            tile = jnp.concatenate(
                [hist, qkv3, jnp.zeros((n_seq, SUBLANES - l - hist.shape[1], 3 * nq), F32)], axis=1)
            tile = tile.reshape(n_seq * SUBLANES, 3 * nq)
            conv_out = _conv(tile, jnp.zeros((n_seq, SUBLANES, 3 * nq), F32), conv_w, n_seq, SUBLANES)
            first = hist.shape[1]

            def one_chunk(a):
                return jnp.pad(a, ((0, 0), (0, SAMPLE_CHUNK - l), (0, 0))).reshape(n_seq * SAMPLE_CHUNK, -1)

            q_dn, k_dn, v_dn = (one_chunk(a.reshape(n_seq, SUBLANES, nq)[:, first:first + l]) for a in conv_out)
            o_dn, s_new = _delta(q_dn, k_dn, v_dn, one_chunk(ab.reshape(n_seq, l, -1)),
                                 one_chunk(z.reshape(n_seq, l, -1)), delta_state[layer], a_log, dt_bias,
                                 dn_gain, n_seq, chunk=SAMPLE_CHUNK, n_chunks=1, valid_rows=l)
            o_dn = o_dn.reshape(n_seq, SAMPLE_CHUNK, nq)[:, :l].reshape(n_seq * l, nq)
            o_sb = _sb_decode(q_sb.reshape(n_seq, l, sb), k_sb.reshape(n_seq, l, sb), v_sb.reshape(n_seq, l, sb),
                              cache_k, cache_v, page_table, layer, sb_bias, sb_gain, pages=8)
            o_sb = o_sb.reshape(n_seq * l, sb)
            conv_new = jnp.concatenate([hist, qkv3], axis=1)[:, -(CONV_W - 1):]
        h = _outproj(o_dn, o_sb, h, norms, w_o[:nq], w_o[nq:], tm)
        h = _ffn(h, norms, weights["ffn2_w_gate_up"][layer].astype(BF16),
                 weights["ffn2_w_down"][layer].astype(BF16), 4, 5, tm)
        if past is not None:
            ks.append(k_sb.reshape(n_seq, l, SB_HEADS, SB_DIM))
            vs.append(v_sb.reshape(n_seq, l, SB_HEADS, SB_DIM))
        ss.append(s_new)
        cs.append(conv_new)
    if past is None:
        k_rows, v_rows = (jnp.transpose(a.reshape(n_seq, depth, SB_HEADS, SB_DIM, l), (0, 4, 1, 2, 3))
                          for a in kv_rows)
    else:
        k_rows, v_rows = jnp.stack(ks, axis=2), jnp.stack(vs, axis=2)
    return h.reshape(n_seq, l, d), k_rows, v_rows, jnp.stack(ss, axis=0), jnp.stack(cs, axis=0)


def kernel(x_prompt, x_sample, cache_sb_k, cache_sb_v, page_table, state_delta, state_conv, norms,
           ffn1_w_gate_up, ffn1_w_down, w_in, conv_w, a_log, dt_bias, dn_out_norm, sb_logit_bias, sb_out_norm,
           w_o, ffn2_w_gate_up, ffn2_w_down):
    weights = dict(norms=norms, ffn1_w_gate_up=ffn1_w_gate_up, ffn1_w_down=ffn1_w_down, w_in=w_in,
                   conv_w=conv_w, a_log=a_log, dt_bias=dt_bias, dn_out_norm=dn_out_norm,
                   sb_logit_bias=sb_logit_bias, sb_out_norm=sb_out_norm, w_o=w_o,
                   ffn2_w_gate_up=ffn2_w_gate_up, ffn2_w_down=ffn2_w_down)
    n_phys, page, depth = cache_sb_k.shape[:3]
    cache_k = jnp.transpose(cache_sb_k, (0, 2, 3, 4, 1)).reshape(n_phys, depth, -1, page)
    cache_v = jnp.transpose(cache_sb_v, (0, 2, 3, 4, 1)).reshape(n_phys, depth, -1, page)
    y_p, k_p, v_p, d_p, c_p = _run_group(x_prompt, weights, tm=512, past=None)
    n_tok = x_sample.shape[0] * x_sample.shape[1]
    y_s, k_s, v_s, d_s, c_s = _run_group(x_sample, weights, tm=n_tok,
                                         past=(state_conv, state_delta, cache_k, cache_v, page_table))
    return (y_p, y_s, k_p, v_p, k_s, v_s, d_p, d_s, c_p, c_s)
```

```python
import functools
import math

import jax
import jax.numpy as jnp
from jax import lax
from jax.experimental import pallas as pl
from jax.experimental.pallas import tpu as pltpu

F32 = jnp.float32
BF16 = jnp.bfloat16
HIGHEST = lax.Precision.HIGHEST

EPS = 1e-6
DN_HEADS = 4
DN_DIM = 128
SB_HEADS = 8
SB_DIM = 64
CONV_W = 4
LANES = 128
SUBLANES = 8
VMEM_LIMIT = 56 * 1024 * 1024
MXU_DIM = 256
SUFFIX_SUM_WIDTH = MXU_DIM
LOG2E = 1.4426950408889634
LOCAL_GROUP = 4
SAMPLE_CHUNK = 16


def _dot(a, b, precision=None):
    return jnp.dot(a, b, preferred_element_type=F32, precision=precision)


def _dot_nt(a, b):
    return lax.dot_general(a, b, (((1,), (1,)), ((), ())), preferred_element_type=F32)


def _dot_tn(a, b):
    return lax.dot_general(a, b, (((0,), (0,)), ((), ())), preferred_element_type=F32)


def _rms(x, gain):
    ms = jnp.mean(x * x, axis=-1, keepdims=True)
    return x * lax.rsqrt(ms + EPS) * gain


def _sigmoid(x):
    return 1.0 / (1.0 + jnp.exp(-x))


def _softplus(x):
    return jnp.maximum(x, 0.0) + jnp.log(1.0 + jnp.exp(-jnp.abs(x)))


def _run_staged(stagers):
    results = [None] * len(stagers)
    live = list(enumerate(stagers))
    while live:
        unfinished = []
        for index, stager in live:
            try:
                next(stager)
                unfinished.append((index, stager))
            except StopIteration as stop:
                results[index] = stop.value
        live = unfinished
    return results


def _params(*sem):
    return pltpu.CompilerParams(dimension_semantics=sem, vmem_limit_bytes=VMEM_LIMIT)


def _ffn_kernel(x_ref, g_ref, wgu_ref, wd_ref, o_ref, acc_ref, *, pre, post, d_ff, chunk):
    x = x_ref[...]
    xn = _rms(x, g_ref[pre:pre + 1, :]).astype(BF16)
    for c in range(d_ff // chunk):
        gate = _dot(xn, wgu_ref[:, c * chunk:(c + 1) * chunk])
        up = _dot(xn, wgu_ref[:, d_ff + c * chunk:d_ff + (c + 1) * chunk])
        h = (gate * _sigmoid(gate) * up).astype(BF16)
        part = _dot(h, wd_ref[c * chunk:(c + 1) * chunk, :])
        if c == 0:
            acc_ref[...] = part
        else:
            acc_ref[...] += part
    o_ref[...] = x + 0.5 * _rms(acc_ref[...], g_ref[post:post + 1, :])


def _ffn(x, norms, w_gu, w_d, pre, post, tm):
    t, d = x.shape
    d_ff = w_d.shape[0]
    const = lambda i: (0, 0)
    return pl.pallas_call(
        functools.partial(_ffn_kernel, pre=pre, post=post, d_ff=d_ff, chunk=256),
        out_shape=jax.ShapeDtypeStruct((t, d), F32),
        grid=(t // tm,),
        in_specs=[pl.BlockSpec((tm, d), lambda i: (i, 0)),
                  pl.BlockSpec(norms.shape, const),
                  pl.BlockSpec(w_gu.shape, const),
                  pl.BlockSpec(w_d.shape, const)],
        out_specs=pl.BlockSpec((tm, d), lambda i: (i, 0)),
        scratch_shapes=[pltpu.VMEM((tm, d), F32)],
        compiler_params=_params("arbitrary"),
        name="ffn",
    )(x, norms, w_gu, w_d)


def _inproj_kernel(*refs, sb, n_prev, feature_major):
    h_ref, g_ref, wqkv_ref, wz_ref, wab_ref, wsb_ref = refs[:6]
    prev = refs[6:8] if n_prev else ()
    qkv_ref, z_ref, ab_ref, q_ref, kb_ref, vb_ref, k_ref, v_ref = refs[6 + len(prev):]
    u = _rms(h_ref[...], g_ref[2:3, :]).astype(BF16)
    qkv_ref[...] = _dot(u, wqkv_ref[...])
    z_ref[...] = _dot(u, wz_ref[...])
    ab_ref[...] = _dot(u, wab_ref[...])
    p = _dot(u, wsb_ref[...])
    k = p[:, sb:2 * sb]
    v = p[:, 2 * sb:]
    q_ref[...] = p[:, :sb]
    kb_ref[...] = k.astype(BF16)
    vb_ref[...] = v.astype(BF16)
    if feature_major:
        for new, old in zip((k_ref, v_ref), prev):
            new[0, 0:n_prev] = old[0]
        k_ref[0, n_prev] = k.T
        v_ref[0, n_prev] = v.T
    else:
        k_ref[...] = k
        v_ref[...] = v


def _inproj(h, norms, w_qkv, w_z, w_ab, w_sb, n_seq, tm, prev_rows):
    t, d = h.shape
    sb = SB_HEADS * SB_DIM
    nt = t // n_seq // tm
    const = lambda b, i: (0, 0)
    row = lambda n: pl.BlockSpec((tm, n), lambda b, i: (b * nt + i, 0))
    widths = (w_qkv.shape[1], w_z.shape[1], LANES, sb, sb, sb)
    dtypes = (F32, F32, F32, F32, BF16, BF16)
    out_shape = [jax.ShapeDtypeStruct((t, n), dt) for n, dt in zip(widths, dtypes)]
    out_specs = [row(n) for n in widths]
    prev = ()
    n_prev = 0
    if prev_rows is None:
        out_shape += [jax.ShapeDtypeStruct((t, sb), F32)] * 2
        out_specs += [row(sb)] * 2
    else:
        if prev_rows[0] is not None:
            prev = tuple(prev_rows)
            n_prev = prev[0].shape[1]
        layers = lambda n: pl.BlockSpec((1, n, sb, tm), lambda b, i: (b, 0, 0, i))
        out_shape += [jax.ShapeDtypeStruct((n_seq, n_prev + 1, sb, t // n_seq), F32)] * 2
        out_specs += [layers(n_prev + 1)] * 2
    return pl.pallas_call(
        functools.partial(_inproj_kernel, sb=sb, n_prev=n_prev, feature_major=prev_rows is not None),
        out_shape=out_shape,
        grid=(n_seq, nt),
        in_specs=[row(d), pl.BlockSpec(norms.shape, const)]
                 + [pl.BlockSpec(w.shape, const) for w in (w_qkv, w_z, w_ab, w_sb)]
                 + [layers(n_prev) for _ in prev],
        out_specs=out_specs,
        compiler_params=_params("arbitrary", "arbitrary"),
        name="inproj",
    )(h, norms, w_qkv, w_z, w_ab, w_sb, *prev)


def _conv_kernel(x_ref, prev_ref, hist_ref, w_ref, q_ref, k_ref, v_ref):
    x = x_ref[...]
    w = w_ref[...]
    prev = jnp.where(pl.program_id(1) == 0, hist_ref[0], prev_ref[...])
    row = lax.broadcasted_iota(jnp.int32, prev.shape, 0)
    y = x * w[CONV_W - 1:CONV_W, :]
    y_head = y[0:SUBLANES]
    for s in range(1, CONV_W):
        tap = w[CONV_W - 1 - s:CONV_W - s, :]
        xs = pltpu.roll(x, s, 0)
        y = y + xs * tap
        head = jnp.where(row < s, pltpu.roll(prev, s, 0), xs[0:SUBLANES])
        y_head = y_head + head * tap

    def finish(y, rows):
        y = y * _sigmoid(y)
        nq = DN_HEADS * DN_DIM
        for h in range(DN_HEADS):
            for ref, off, scale in ((q_ref, 0, DN_DIM ** -0.5), (k_ref, nq, 1.0)):
                t = y[:, off + h * DN_DIM:off + (h + 1) * DN_DIM]
                ss = jnp.sum(t * t, axis=-1, keepdims=True)
                ref[rows, h * DN_DIM:(h + 1) * DN_DIM] = t * (lax.rsqrt(ss + EPS) * scale)
        v_ref[rows, :] = y[:, 2 * nq:]

    finish(y, slice(None))
    finish(y_head, slice(0, SUBLANES))


def _conv(x, hist8, w, n_seq, tm):
    t, c = x.shape
    nt = t // n_seq // tm
    blocks8 = tm // SUBLANES
    n = DN_HEADS * DN_DIM
    out = pl.BlockSpec((tm, n), lambda b, i: (b * nt + i, 0))
    return pl.pallas_call(
        _conv_kernel,
        out_shape=[jax.ShapeDtypeStruct((t, n), F32)] * 3,
        grid=(n_seq, nt),
        in_specs=[pl.BlockSpec((tm, c), lambda b, i: (b * nt + i, 0)),
                  pl.BlockSpec((SUBLANES, c), lambda b, i: (jnp.maximum((b * nt + i) * blocks8 - 1, 0), 0)),
                  pl.BlockSpec((1, SUBLANES, c), lambda b, i: (b, 0, 0)),
                  pl.BlockSpec(w.shape, lambda b, i: (0, 0))],
        out_specs=[out, out, out],
        compiler_params=_params("arbitrary", "arbitrary"),
        name="conv",
    )(x, x, hist8, w)


def _split(x, terms):
    parts = []
    for _ in range(terms):
        p = x.astype(BF16)
        parts.append(p)
        x = x - p.astype(F32)
    return parts


def _dot_left01(a01, b):
    return sum(_dot(a01, p) for p in _split(b, 3))


def _dot_split(lefts, b):
    b_hi, b_lo = _split(b, 2)
    out = []
    for a in lefts:
        a_hi, a_lo = _split(a, 2)
        out.append(_dot(a_hi, b_hi) + _dot(a_lo, b_hi) + _dot(a_hi, b_lo))
    return out


def _delta_kernel(q_ref, k_ref, v_ref, ab_ref, z_ref, s0_ref, alog_ref, dtb_ref, gain_ref,
                  o_ref, s_ref, u_ref, w_ref, qk_ref, qe_ref, kd_ref, decay_last_ref,
                  *, chunk, n_chunks, valid_rows):
    c = chunk

    @pl.when(pl.program_id(1) == 0)
    def _():
        s_ref[...] = s0_ref[...]

    r = DN_HEADS * c
    ri = lax.broadcasted_iota(jnp.int32, (r, r), 0)
    ci = lax.broadcasted_iota(jnp.int32, (r, r), 1)
    same_head = (ri // c) == (ci // c)
    incl = same_head & (ri >= ci)
    strict = same_head & (ri > ci)
    eye = ri == ci
    ones = jnp.ones((SUBLANES, r), BF16)
    ltri =(lax.broadcasted_iota(jnp.int32, (c, c), 0) >= lax.broadcasted_iota(jnp.int32, (c, c), 1)).astype(BF16)
    neg_a = -jnp.exp(alog_ref[...])
    dtb = dtb_ref[...]
    gain = gain_ref[...]
    n_doublings = max(c.bit_length() - 1, 0)
    heads = range(DN_HEADS)

    def head_cols(h):
        return slice(h * DN_DIM, (h + 1) * DN_DIM)

    def stacked_rows(ch):
        return pl.ds(pl.multiple_of(ch * r, r), r)

    def local(ch):
        rows = pl.ds(pl.multiple_of(ch * c, c), c)
        gb = ab_ref[rows, :]
        g_all = neg_a * _softplus(gb + dtb)
        beta_all = _sigmoid(gb)
        if valid_rows < c:
            live = lax.broadcasted_iota(jnp.int32, g_all.shape, 0) < valid_rows
            g_all = jnp.where(live, g_all, 0.0)
            beta_all = jnp.where(live, beta_all, 0.0)
        gcum_all = _dot_left01(ltri, g_all)
        yield
        decay_last_ref[pl.ds(pl.multiple_of(ch * SUBLANES, SUBLANES), SUBLANES), :] = jnp.broadcast_to(
            jnp.exp(gcum_all[c - 1:c, :]), (SUBLANES, LANES))
        beta = jnp.concatenate([beta_all[:, DN_HEADS + h:DN_HEADS + h + 1] for h in heads], axis=0)
        gcum = jnp.concatenate([gcum_all[:, h:h + 1] for h in heads], axis=0)
        g_last = jnp.concatenate([jnp.broadcast_to(gcum_all[c - 1:c, h:h + 1], (c, 1)) for h in heads], axis=0)
        gcum_cols = jnp.broadcast_to(gcum, (r, r))
        gcum_row = _dot_left01(ones, jnp.where(eye, gcum_cols, 0.0))[0:1, :]
        yield
        decay = jnp.where(incl, jnp.exp(jnp.minimum(gcum_cols - gcum_row, 0.0)), 0.0)
        q = jnp.concatenate([q_ref[rows, head_cols(h)] for h in heads], axis=0)
        k = jnp.concatenate([k_ref[rows, head_cols(h)] for h in heads], axis=0)
        v = jnp.concatenate([v_ref[rows, head_cols(h)] for h in heads], axis=0)
        kb = k * beta
        with_k = _dot_nt(jnp.concatenate([kb, q], axis=0).astype(BF16), k.astype(BF16))
        yield
        m = jnp.where(strict, with_k[:r] * decay, 0.0)
        out = stacked_rows(ch)
        qk_ref[out, :] = (with_k[r:] * decay).astype(BF16)
        inv = jnp.where(eye, 1.0, 0.0) - m
        pw, = _dot_split([m], m)
        yield
        for step in range(1, n_doublings):
            if step + 1 < n_doublings:
                grown, pw = _dot_split([inv, pw], pw)
            else:
                grown, = _dot_split([inv], pw)
            inv = inv + grown
            yield
        e_gcum = jnp.exp(gcum)
        sol, = _dot_split([inv], jnp.concatenate([v * beta, kb * e_gcum], axis=1))
        u_ref[out, :] = sol[:, :DN_DIM]
        w_ref[out, :] = sol[:, DN_DIM:].astype(BF16)
        qe_ref[out, :] = (q * e_gcum).astype(BF16)
        kd_ref[out, :] = (k * jnp.exp(g_last - gcum)).astype(BF16)

    group = math.gcd(n_chunks, LOCAL_GROUP)

    def local_group(i, carry):
        _run_staged([local(group * i + g) for g in range(group)])
        return carry

    lax.fori_loop(0, n_chunks // group, local_group, 0)

    def scan(ch, carry):
        rows = pl.ds(pl.multiple_of(ch * c, c), c)
        decay_last = decay_last_ref[pl.ds(pl.multiple_of(ch * SUBLANES, SUBLANES), SUBLANES), :]
        s = [s_ref[0, h] for h in heads]
        s16 = [x.astype(BF16) for x in s]
        head_rows = [pl.ds(pl.multiple_of(ch * r + h * c, c), c) for h in heads]
        u16 = [(u_ref[head_rows[h], :] - _dot(w_ref[head_rows[h], :], s16[h])).astype(BF16) for h in heads]
        u16_all = jnp.concatenate(u16, axis=0)
        for h in heads:
            o = _dot(qe_ref[head_rows[h], :], s16[h]) + _dot(qk_ref[head_rows[h], :], u16_all)
            s_ref[0, h] = s[h] * decay_last[0:1, h:h + 1] + _dot_tn(kd_ref[head_rows[h], :], u16[h])
            zz = z_ref[rows, head_cols(h)]
            o_ref[rows, head_cols(h)] = _rms(o, gain) * (zz * _sigmoid(zz))
        return carry

    lax.fori_loop(0, n_chunks, scan, 0)


def _delta(q, k, v, ab, z, s0, a_log, dt_bias, gain, n_seq, chunk, n_chunks, valid_rows):
    t, n = q.shape
    rows = chunk * n_chunks
    stacked = DN_HEADS * rows
    steps = t // n_seq // rows
    blk = lambda w: pl.BlockSpec((rows, w), lambda b, i: (b * steps + i, 0))
    state = pl.BlockSpec((1,) + s0.shape[1:], lambda b, i: (b, 0, 0, 0))
    vec = pl.BlockSpec((1, LANES), lambda b, i: (0, 0))
    return pl.pallas_call(
        functools.partial(_delta_kernel, chunk=chunk, n_chunks=n_chunks, valid_rows=valid_rows),
        out_shape=[jax.ShapeDtypeStruct((t, n), F32), jax.ShapeDtypeStruct(s0.shape, F32)],
        grid=(n_seq, steps),
        in_specs=[blk(n), blk(n), blk(n), blk(LANES), blk(n), state, vec, vec, vec],
        out_specs=[blk(n), state],
        scratch_shapes=[pltpu.VMEM((stacked, DN_DIM), F32), pltpu.VMEM((stacked, DN_DIM), BF16),
                        pltpu.VMEM((stacked, DN_HEADS * chunk), BF16),
                        pltpu.VMEM((stacked, DN_DIM), BF16), pltpu.VMEM((stacked, DN_DIM), BF16),
                        pltpu.VMEM((n_chunks * SUBLANES, LANES), F32)],
        compiler_params=_params("arbitrary", "arbitrary"),
        name="delta",
    )(q, k, v, ab, z, s0, a_log, dt_bias, gain)


def _softplus2(x):
    return jnp.maximum(x, 0.0) + jnp.log(1.0 + jnp.exp2(jnp.minimum(x, -x))) * LOG2E


def _sb_weights(z2, later, usum, valid=None):
    n = usum.shape[0]
    sp = _softplus2(z2)
    if valid is not None:
        sp = jnp.where(valid, sp, 0.0)
    pieces = []
    for b in reversed(range(z2.shape[1] // n)):
        cols = slice(b * n, (b + 1) * n)
        within = _dot(sp[:, cols].astype(BF16), usum)
        pieces.append(jnp.exp2(z2[:, cols] - within - later))
        later = later + jnp.sum(sp[:, cols], axis=-1, keepdims=True)
    a = pieces[0] if len(pieces) == 1 else jnp.concatenate(pieces[::-1], axis=1)
    if valid is not None:
        a = jnp.where(valid, a, 0.0)
    return a.astype(BF16), later


def _suffix_sum_matrix(n):
    j = lax.broadcasted_iota(jnp.int32, (n, n), 0)
    s = lax.broadcasted_iota(jnp.int32, (n, n), 1)
    return (j >= s).astype(BF16)


def _sbp_kernel(bias_ref, q_ref, k_ref, v_ref, gain_ref, usum_ref, bd_ref, o_ref, *, tq):
    pair = pl.program_id(1)
    i = pl.program_id(2)
    lane = lax.broadcasted_iota(jnp.int32, (tq, LANES), 1)
    first = lane < SB_DIM
    q = q_ref[0] * (SB_DIM ** -0.5 * LOG2E)
    q_heads = (jnp.where(first, q, 0.0).astype(BF16), jnp.where(first, 0.0, q).astype(BF16))
    bias = (bias_ref[2 * pair] * LOG2E, bias_ref[2 * pair + 1] * LOG2E)
    usum = usum_ref[...]
    causal = lax.broadcasted_iota(jnp.int32, (tq, tq), 1) < lax.broadcasted_iota(jnp.int32, (tq, tq), 0)

    def block(j, carry, valid):
        rows = pl.ds(pl.multiple_of(j * tq, tq), tq)
        ks = k_ref[0, rows, :]
        vs = v_ref[0, rows, :]
        out = []
        for hh in range(2):
            later, acc = carry[hh]
            z2 = _dot_nt(q_heads[hh], ks) + bias[hh]
            a, later = _sb_weights(z2, later, usum, valid)
            out.append((later, acc + _dot(a, vs)))
        return tuple(out)

    zero = (jnp.zeros((tq, 1), F32), jnp.zeros((tq, LANES), F32))
    carry = block(i, (zero, zero), causal)
    carry = lax.fori_loop(0, i, lambda t, c: block(i - 1 - t, c, None), carry)
    o = jnp.where(first, carry[0][1], carry[1][1])
    ss = _dot(o * o, bd_ref[...], HIGHEST)
    o_ref[0] = o * lax.rsqrt(ss * (1.0 / SB_DIM) + EPS) * gain_ref[...]


def _sb_prompt(q, k16, v16, bias, gain2, n_seq, tq):
    t, n = q.shape
    l = t // n_seq
    pairs = n // LANES
    q3 = q.reshape(n_seq, l, n)
    k3 = k16.reshape(n_seq, l, n)
    v3 = v16.reshape(n_seq, l, n)
    seg = jnp.arange(LANES) // SB_DIM
    bd = (seg[:, None] == seg[None, :]).astype(F32)
    usum = _suffix_sum_matrix(SUFFIX_SUM_WIDTH)
    kv = pl.BlockSpec((1, l, LANES), lambda b, p, i: (b, 0, p))
    const = lambda b, p, i: (0, 0)
    out = pl.pallas_call(
        functools.partial(_sbp_kernel, tq=tq),
        out_shape=jax.ShapeDtypeStruct((n_seq, l, n), F32),
        grid=(n_seq, pairs, l // tq),
        in_specs=[pl.BlockSpec(memory_space=pltpu.SMEM),
                  pl.BlockSpec((1, tq, LANES), lambda b, p, i: (b, i, p)),
                  kv, kv,
                  pl.BlockSpec((1, LANES), const),
                  pl.BlockSpec(usum.shape, const),
                  pl.BlockSpec(bd.shape, const)],
        out_specs=pl.BlockSpec((1, tq, LANES), lambda b, p, i: (b, i, p)),
        compiler_params=_params("arbitrary", "arbitrary", "arbitrary"),
        name="sb_prompt",
    )(bias, q3, k3, v3, gain2, usum, bd)
    return out.reshape(t, n)


def _sbd_kernel(pt_ref, q_ref, bias_ref, knew_ref, vnew_ref, gain_ref, usum_ref, *rest, pages, n_new):
    k_pages = rest[:pages]
    v_pages = rest[pages:2 * pages]
    o_ref = rest[2 * pages]
    later_ref, acc_ref, kpad_ref, vpad_ref = rest[2 * pages + 1:]
    step = pl.program_id(1)
    n_rows, width = acc_ref.shape
    page = kpad_ref.shape[0]
    own_head = (lax.broadcasted_iota(jnp.int32, (n_rows, width), 1) // SB_DIM
                == lax.broadcasted_iota(jnp.int32, (n_rows, width), 0) % SB_HEADS)
    q = jnp.where(own_head, q_ref[0] * (SB_DIM ** -0.5 * LOG2E), 0.0).astype(BF16)
    bias = bias_ref[...] * LOG2E
    usum = usum_ref[...]

    def block(z2, weighted_sum, valid):
        a, later = _sb_weights(z2 + bias, later_ref[...], usum, valid)
        later_ref[...] = later
        acc_ref[...] += weighted_sum(a)

    @pl.when(step == 0)
    def _():
        later_ref[...] = jnp.zeros_like(later_ref)
        acc_ref[...] = jnp.zeros_like(acc_ref)
        kpad_ref[...] = jnp.zeros_like(kpad_ref)
        vpad_ref[...] = jnp.zeros_like(vpad_ref)
        kpad_ref[0:n_new, :] = knew_ref[0]
        vpad_ref[0:n_new, :] = vnew_ref[0]
        key = lax.broadcasted_iota(jnp.int32, (n_rows, page), 1)
        tok = lax.broadcasted_iota(jnp.int32, (n_rows, page), 0) // SB_HEADS
        vs = vpad_ref[...].astype(BF16)
        block(_dot_nt(q, kpad_ref[...].astype(BF16)), lambda a: _dot(a, vs), key < tok)

    def weighted_sum(a):
        return sum(_dot_nt(a[:, i * page:(i + 1) * page], v_pages[i][0, 0].astype(BF16)) for i in range(pages))

    block(jnp.concatenate([_dot(q, k_pages[i][0, 0].astype(BF16)) for i in range(pages)], axis=1),
          weighted_sum, None)

    @pl.when(step == pl.num_programs(1) - 1)
    def _():
        o = jnp.where(own_head, acc_ref[...], 0.0)
        ss = jnp.sum(o * o, axis=-1, keepdims=True)
        o = o * lax.rsqrt(ss * (1.0 / SB_DIM) + EPS)
        o = jnp.sum(o.reshape(n_rows // SB_HEADS, SB_HEADS, width), axis=1)
        o_ref[0] = o * gain_ref[...]


def _sb_decode(q, k_new, v_new, cache_k, cache_v, page_table, layer, bias, gain, pages):
    n_seq, n_new, n = q.shape
    page = cache_k.shape[3]
    n_pages = page_table.shape[1]
    steps = n_pages // pages
    n_rows = n_new * SB_HEADS
    q_rep = jnp.repeat(q, SB_HEADS, axis=1)
    bias_col = jnp.tile(bias, n_new).reshape(n_rows, 1)
    gain_row = jnp.tile(gain, SB_HEADS).reshape(1, n)
    usum = _suffix_sum_matrix(page)

    def page_spec(i):
        return pl.BlockSpec((1, 1, n, page),
                            lambda b, s, pt: (pt[b, (steps - 1 - s) * pages + i], layer, 0, 0))

    per_seq = lambda r: pl.BlockSpec((1, r, n), lambda b, s, pt: (b, 0, 0))
    const = lambda b, s, pt: (0, 0)
    grid_spec = pltpu.PrefetchScalarGridSpec(
        num_scalar_prefetch=1,
        grid=(n_seq, steps),
        in_specs=[per_seq(n_rows),
                  pl.BlockSpec(bias_col.shape, const),
                  per_seq(n_new), per_seq(n_new),
                  pl.BlockSpec(gain_row.shape, const),
                  pl.BlockSpec(usum.shape, const)]
                 + [page_spec(i) for i in range(pages)] * 2,
        out_specs=per_seq(n_new),
        scratch_shapes=[pltpu.VMEM((n_rows, 1), F32), pltpu.VMEM((n_rows, n), F32),
                        pltpu.VMEM((page, n), F32), pltpu.VMEM((page, n), F32)],
    )
    return pl.pallas_call(
        functools.partial(_sbd_kernel, pages=pages, n_new=n_new),
        out_shape=jax.ShapeDtypeStruct((n_seq, n_new, n), F32),
        grid_spec=grid_spec,
        compiler_params=_params("arbitrary", "arbitrary"),
        name="sb_decode",
    )(page_table, q_rep, bias_col, k_new, v_new, gain_row, usum,
      *([cache_k] * pages), *([cache_v] * pages))


def _outproj_kernel(a_ref, b_ref, h_ref, g_ref, wa_ref, wb_ref, o_ref):
    mix = _dot(a_ref[...].astype(BF16), wa_ref[...]) + _dot(b_ref[...].astype(BF16), wb_ref[...])
    o_ref[...] = h_ref[...] + _rms(mix, g_ref[3:4, :])


def _outproj(a, b, h, norms, w_a, w_b, tm):
    t, d = h.shape
    const = lambda i: (0, 0)
    row = lambda n: pl.BlockSpec((tm, n), lambda i: (i, 0))
    return pl.pallas_call(
        _outproj_kernel,
        out_shape=jax.ShapeDtypeStruct((t, d), F32),
        grid=(t // tm,),
        in_specs=[row(a.shape[1]), row(b.shape[1]), row(d), pl.BlockSpec(norms.shape, const),
                  pl.BlockSpec(w_a.shape, const), pl.BlockSpec(w_b.shape, const)],
        out_specs=row(d),
        compiler_params=_params("arbitrary"),
        name="outproj",
    )(a, b, h, norms, w_a, w_b)


def _pad_lanes(v):
    return jnp.pad(v.astype(F32), (0, LANES - v.shape[0])).reshape(1, LANES)


def _run_group(x, weights, *, tm, past):
    n_seq, l, d = x.shape
    depth = weights["norms"].shape[0]
    nq = DN_HEADS * DN_DIM
    sb = SB_HEADS * SB_DIM
    h = x.reshape(n_seq * l, d)
    ks, vs, ss, cs = [], [], [], []
    kv_rows = (None, None)
    for layer in range(depth):
        norms = weights["norms"][layer]
        w_in = weights["w_in"][layer].astype(BF16)
        w_qkv = w_in[:, :3 * nq]
        w_z = w_in[:, 3 * nq:4 * nq]
        w_ab = jnp.pad(w_in[:, 4 * nq:4 * nq + 2 * DN_HEADS], ((0, 0), (0, LANES - 2 * DN_HEADS)))
        w_sb = w_in[:, 4 * nq + 2 * DN_HEADS:]
        w_o = weights["w_o"][layer].astype(BF16)
        a_log = _pad_lanes(weights["a_log"][layer])
        dt_bias = _pad_lanes(weights["dt_bias"][layer])
        dn_gain = weights["dn_out_norm"][layer].reshape(1, DN_DIM)
        sb_bias = weights["sb_logit_bias"][layer]
        sb_gain = weights["sb_out_norm"][layer]

        h = _ffn(h, norms, weights["ffn1_w_gate_up"][layer].astype(BF16),
                 weights["ffn1_w_down"][layer].astype(BF16), 0, 1, tm)
        qkv, z, ab, q_sb, k16, v16, k_sb, v_sb = _inproj(
            h, norms, w_qkv, w_z, w_ab, w_sb, 1 if past else n_seq, tm, None if past else kv_rows)
        conv_w = weights["conv_w"][layer]
        if past is None:
            kv_rows = (k_sb, v_sb)
            hist8 = jnp.zeros((n_seq, SUBLANES, 3 * nq), F32)
            q_dn, k_dn, v_dn = _conv(qkv, hist8, conv_w, n_seq, tm)
            s0 = jnp.zeros((n_seq, DN_HEADS, DN_DIM, DN_DIM), F32)
            o_dn, s_new = _delta(q_dn, k_dn, v_dn, ab, z, s0, a_log, dt_bias, dn_gain,
                                 n_seq, chunk=64, n_chunks=8, valid_rows=64)
            o_sb = _sb_prompt(q_sb, k16, v16, sb_bias, jnp.tile(sb_gain, 2).reshape(1, LANES), n_seq, tq=512)
            conv_new = qkv.reshape(n_seq, l, 3 * nq)[:, l - (CONV_W - 1):]
        else:
            conv_state, delta_state, cache_k, cache_v, page_table = past
            qkv3 = qkv.reshape(n_seq, l, 3 * nq)
            hist = conv_state[layer]
            tile = jnp.concatenate(
                [hist, qkv3, jnp.zeros((n_seq, SUBLANES - l - hist.shape[1], 3 * nq), F32)], axis=1)
            tile = tile.reshape(n_seq * SUBLANES, 3 * nq)
            conv_out = _conv(tile, jnp.zeros((n_seq, SUBLANES, 3 * nq), F32), conv_w, n_seq, SUBLANES)
            first = hist.shape[1]

            def one_chunk(a):
                return jnp.pad(a, ((0, 0), (0, SAMPLE_CHUNK - l), (0, 0))).reshape(n_seq * SAMPLE_CHUNK, -1)

            q_dn, k_dn, v_dn = (one_chunk(a.reshape(n_seq, SUBLANES, nq)[:, first:first + l]) for a in conv_out)
            o_dn, s_new = _delta(q_dn, k_dn, v_dn, one_chunk(ab.reshape(n_seq, l, -1)),
                                 one_chunk(z.reshape(n_seq, l, -1)), delta_state[layer], a_log, dt_bias,
                                 dn_gain, n_seq, chunk=SAMPLE_CHUNK, n_chunks=1, valid_rows=l)
            o_dn = o_dn.reshape(n_seq, SAMPLE_CHUNK, nq)[:, :l].reshape(n_seq * l, nq)
            o_sb = _sb_decode(q_sb.reshape(n_seq, l, sb), k_sb.reshape(n_seq, l, sb), v_sb.reshape(n_seq, l, sb),
                              cache_k, cache_v, page_table, layer, sb_bias, sb_gain, pages=16)
            o_sb = o_sb.reshape(n_seq * l, sb)
            conv_new = jnp.concatenate([hist, qkv3], axis=1)[:, -(CONV_W - 1):]
        h = _outproj(o_dn, o_sb, h, norms, w_o[:nq], w_o[nq:], tm)
        h = _ffn(h, norms, weights["ffn2_w_gate_up"][layer].astype(BF16),
                 weights["ffn2_w_down"][layer].astype(BF16), 4, 5, tm)
        if past is not None:
            ks.append(k_sb.reshape(n_seq, l, SB_HEADS, SB_DIM))
            vs.append(v_sb.reshape(n_seq, l, SB_HEADS, SB_DIM))
        ss.append(s_new)
        cs.append(conv_new)
    if past is None:
        k_rows, v_rows = (jnp.transpose(a.reshape(n_seq, depth, SB_HEADS, SB_DIM, l), (0, 4, 1, 2, 3))
                          for a in kv_rows)
    else:
        k_rows, v_rows = jnp.stack(ks, axis=2), jnp.stack(vs, axis=2)
    return h.reshape(n_seq, l, d), k_rows, v_rows, jnp.stack(ss, axis=0), jnp.stack(cs, axis=0)


def kernel(x_prompt, x_sample, cache_sb_k, cache_sb_v, page_table, state_delta, state_conv, norms,
           ffn1_w_gate_up, ffn1_w_down, w_in, conv_w, a_log, dt_bias, dn_out_norm, sb_logit_bias, sb_out_norm,
           w_o, ffn2_w_gate_up, ffn2_w_down):
    weights = dict(norms=norms, ffn1_w_gate_up=ffn1_w_gate_up, ffn1_w_down=ffn1_w_down, w_in=w_in,
                   conv_w=conv_w, a_log=a_log, dt_bias=dt_bias, dn_out_norm=dn_out_norm,
                   sb_logit_bias=sb_logit_bias, sb_out_norm=sb_out_norm, w_o=w_o,
                   ffn2_w_gate_up=ffn2_w_gate_up, ffn2_w_down=ffn2_w_down)
    n_phys, page, depth = cache_sb_k.shape[:3]
    cache_k = jnp.transpose(cache_sb_k, (0, 2, 3, 4, 1)).reshape(n_phys, depth, -1, page)
    cache_v = jnp.transpose(cache_sb_v, (0, 2, 3, 4, 1)).reshape(n_phys, depth, -1, page)
    y_p, k_p, v_p, d_p, c_p = _run_group(x_prompt, weights, tm=512, past=None)
    n_tok = x_sample.shape[0] * x_sample.shape[1]
    y_s, k_s, v_s, d_s, c_s = _run_group(x_sample, weights, tm=n_tok,
                                         past=(state_conv, state_delta, cache_k, cache_v, page_table))
    return (y_p, y_s, k_p, v_p, k_s, v_s, d_p, d_s, c_p, c_s)
```

```python
import functools
import math

import jax
import jax.numpy as jnp
from jax import lax
from jax.experimental import pallas as pl
from jax.experimental.pallas import tpu as pltpu

F32 = jnp.float32
BF16 = jnp.bfloat16
HIGHEST = lax.Precision.HIGHEST

EPS = 1e-6
DN_HEADS = 4
DN_DIM = 128
SB_HEADS = 8
SB_DIM = 64
CONV_W = 4
LANES = 128
SUBLANES = 8
VMEM_LIMIT = 56 * 1024 * 1024
MXU_DIM = 256
SUFFIX_SUM_WIDTH = MXU_DIM
LOG2E = 1.4426950408889634
LOCAL_GROUP = 4
SAMPLE_CHUNK = 16


def _dot(a, b, precision=None):
    return jnp.dot(a, b, preferred_element_type=F32, precision=precision)


def _dot_nt(a, b):
    return lax.dot_general(a, b, (((1,), (1,)), ((), ())), preferred_element_type=F32)


def _dot_tn(a, b):
    return lax.dot_general(a, b, (((0,), (0,)), ((), ())), preferred_element_type=F32)


def _rms(x, gain):
    ms = jnp.mean(x * x, axis=-1, keepdims=True)
    return x * lax.rsqrt(ms + EPS) * gain


def _sigmoid(x):
    return 1.0 / (1.0 + jnp.exp(-x))


def _softplus(x):
    return jnp.maximum(x, 0.0) + jnp.log(1.0 + jnp.exp(-jnp.abs(x)))


def _run_staged(stagers):
    results = [None] * len(stagers)
    live = list(enumerate(stagers))
    while live:
        unfinished = []
        for index, stager in live:
            try:
                next(stager)
                unfinished.append((index, stager))
            except StopIteration as stop:
                results[index] = stop.value
        live = unfinished
    return results


def _params(*sem):
    return pltpu.CompilerParams(dimension_semantics=sem, vmem_limit_bytes=VMEM_LIMIT)


def _ffn_kernel(*refs, pre, post, d_ff, chunk, mixed):
    if mixed:
        a_ref, b_ref, wa_ref, wb_ref, x_ref, g_ref, wgu_ref, wd_ref, o_ref, acc_ref = refs
        mix = _dot(a_ref[...].astype(BF16), wa_ref[...]) + _dot(b_ref[...].astype(BF16), wb_ref[...])
        x = x_ref[...] + _rms(mix, g_ref[pre - 1:pre, :])
    else:
        x_ref, g_ref, wgu_ref, wd_ref, o_ref, acc_ref = refs
        x = x_ref[...]
    xn = _rms(x, g_ref[pre:pre + 1, :]).astype(BF16)
    for c in range(d_ff // chunk):
        gate = _dot(xn, wgu_ref[:, c * chunk:(c + 1) * chunk])
        up = _dot(xn, wgu_ref[:, d_ff + c * chunk:d_ff + (c + 1) * chunk])
        h = (gate * _sigmoid(gate) * up).astype(BF16)
        part = _dot(h, wd_ref[c * chunk:(c + 1) * chunk, :])
        if c == 0:
            acc_ref[...] = part
        else:
            acc_ref[...] += part
    o_ref[...] = x + 0.5 * _rms(acc_ref[...], g_ref[post:post + 1, :])


def _ffn(x, norms, w_gu, w_d, pre, post, tm, mix=None):
    t, d = x.shape
    d_ff = w_d.shape[0]
    const = lambda i: (0, 0)
    row = lambda n: pl.BlockSpec((tm, n), lambda i: (i, 0))
    whole = lambda w: pl.BlockSpec(w.shape, const, pipeline_mode=pl.Buffered(1))
    extra = ()
    extra_specs = []
    if mix is not None:
        a, b, w_a, w_b = mix
        extra = (a, b, w_a, w_b)
        extra_specs = [row(a.shape[1]), row(b.shape[1]), whole(w_a), whole(w_b)]
    return pl.pallas_call(
        functools.partial(_ffn_kernel, pre=pre, post=post, d_ff=d_ff, chunk=256, mixed=mix is not None),
        out_shape=jax.ShapeDtypeStruct((t, d), F32),
        grid=(t // tm,),
        in_specs=extra_specs + [row(d), whole(norms), whole(w_gu), whole(w_d)],
        out_specs=row(d),
        scratch_shapes=[pltpu.VMEM((tm, d), F32)],
        compiler_params=_params("arbitrary"),
        name="ffn",
    )(*extra, x, norms, w_gu, w_d)


def _inproj_kernel(*refs, sb, n_prev, feature_major):
    refs = list(refs)
    h_ref, g_ref, wqkv_ref, wz_ref, wab_ref, wsb_ref = refs[:6]
    del refs[:6]
    if feature_major:
        convw_ref = refs.pop(0)
    prev = refs[:2] if n_prev else ()
    del refs[:len(prev)]
    if feature_major:
        qdn_ref, kdn_ref, vdn_ref, tail_ref, z_ref, ab_ref, q_ref, kb_ref, vb_ref, k_ref, v_ref, carry_ref = refs
    else:
        qkv_ref, z_ref, ab_ref, q_ref, kb_ref, vb_ref, k_ref, v_ref = refs
    u = _rms(h_ref[...], g_ref[2:3, :]).astype(BF16)
    qkv = _dot(u, wqkv_ref[...])
    if feature_major:
        @pl.when(pl.program_id(1) == 0)
        def _():
            carry_ref[...] = jnp.zeros_like(carry_ref)

        _conv_heads(qkv, carry_ref[...], convw_ref[...], qdn_ref, kdn_ref, vdn_ref)
        last = qkv[qkv.shape[0] - SUBLANES:]
        carry_ref[...] = last
        tail_ref[0] = last
    else:
        qkv_ref[...] = qkv
    z_ref[...] = _dot(u, wz_ref[...])
    ab_ref[...] = _dot(u, wab_ref[...])
    p = _dot(u, wsb_ref[...])
    k = p[:, sb:2 * sb]
    v = p[:, 2 * sb:]
    q_ref[...] = p[:, :sb]
    kb_ref[...] = k.astype(BF16)
    vb_ref[...] = v.astype(BF16)
    if feature_major:
        for new, old in zip((k_ref, v_ref), prev):
            new[0, 0:n_prev] = old[0]
        k_ref[0, n_prev] = k.T
        v_ref[0, n_prev] = v.T
    else:
        k_ref[...] = k
        v_ref[...] = v


def _inproj(h, norms, w_qkv, w_z, w_ab, w_sb, n_seq, tm, fresh=None):
    t, d = h.shape
    sb = SB_HEADS * SB_DIM
    nt = t // n_seq // tm
    const = lambda b, i: (0, 0)
    row = lambda n: pl.BlockSpec((tm, n), lambda b, i: (b * nt + i, 0))
    struct = lambda n, dt=F32: jax.ShapeDtypeStruct((t, n), dt)
    whole = lambda w: pl.BlockSpec(w.shape, const)
    n_qkv = w_qkv.shape[1]
    common_shape = [struct(w_z.shape[1]), struct(LANES), struct(sb), struct(sb, BF16), struct(sb, BF16)]
    common_specs = [row(w_z.shape[1]), row(LANES), row(sb), row(sb), row(sb)]
    inputs = [h, norms, w_qkv, w_z, w_ab, w_sb]
    in_specs = [row(d)] + [whole(w) for w in inputs[1:]]
    scratch = []
    n_prev = 0
    if fresh is None:
        out_shape = [struct(n_qkv)] + common_shape + [struct(sb)] * 2
        out_specs = [row(n_qkv)] + common_specs + [row(sb)] * 2
    else:
        conv_w, *prev = fresh
        per_seq = pl.BlockSpec((1, SUBLANES, n_qkv), lambda b, i: (b, 0, 0))
        layers = lambda n: pl.BlockSpec((1, n, sb, tm), lambda b, i: (b, 0, 0, i))
        inputs += [conv_w]
        in_specs += [whole(conv_w)]
        if prev[0] is not None:
            n_prev = prev[0].shape[1]
            inputs += prev
            in_specs += [layers(n_prev)] * 2
        nq = n_qkv // 3
        rows_shape = jax.ShapeDtypeStruct((n_seq, n_prev + 1, sb, t // n_seq), F32)
        out_shape = ([struct(nq)] * 3 + [jax.ShapeDtypeStruct((n_seq, SUBLANES, n_qkv), F32)] + common_shape
                     + [rows_shape] * 2)
        out_specs = [row(nq)] * 3 + [per_seq] + common_specs + [layers(n_prev + 1)] * 2
        scratch = [pltpu.VMEM((SUBLANES, n_qkv), F32)]
    return pl.pallas_call(
        functools.partial(_inproj_kernel, sb=sb, n_prev=n_prev, feature_major=fresh is not None),
        out_shape=out_shape,
        grid=(n_seq, nt),
        in_specs=in_specs,
        out_specs=out_specs,
        scratch_shapes=scratch,
        compiler_params=_params("arbitrary", "arbitrary"),
        name="inproj",
    )(*inputs)


def _conv_heads(x, prev, w, q_ref, k_ref, v_ref):
    row = lax.broadcasted_iota(jnp.int32, prev.shape, 0)
    y = x * w[CONV_W - 1:CONV_W, :]
    y_head = y[0:SUBLANES]
    for s in range(1, CONV_W):
        tap = w[CONV_W - 1 - s:CONV_W - s, :]
        xs = pltpu.roll(x, s, 0)
        y = y + xs * tap
        head = jnp.where(row < s, pltpu.roll(prev, s, 0), xs[0:SUBLANES])
        y_head = y_head + head * tap

    def finish(y, rows):
        y = y * _sigmoid(y)
        nq = DN_HEADS * DN_DIM
        for h in range(DN_HEADS):
            for ref, off, scale in ((q_ref, 0, DN_DIM ** -0.5), (k_ref, nq, 1.0)):
                t = y[:, off + h * DN_DIM:off + (h + 1) * DN_DIM]
                ss = jnp.sum(t * t, axis=-1, keepdims=True)
                ref[rows, h * DN_DIM:(h + 1) * DN_DIM] = t * (lax.rsqrt(ss + EPS) * scale)
        v_ref[rows, :] = y[:, 2 * nq:]

    finish(y, slice(None))
    finish(y_head, slice(0, SUBLANES))


def _conv_kernel(x_ref, prev_ref, hist_ref, w_ref, q_ref, k_ref, v_ref):
    prev = jnp.where(pl.program_id(1) == 0, hist_ref[0], prev_ref[...])
    _conv_heads(x_ref[...], prev, w_ref[...], q_ref, k_ref, v_ref)


def _conv(x, hist8, w, n_seq, tm):
    t, c = x.shape
    nt = t // n_seq // tm
    blocks8 = tm // SUBLANES
    n = DN_HEADS * DN_DIM
    out = pl.BlockSpec((tm, n), lambda b, i: (b * nt + i, 0))
    return pl.pallas_call(
        _conv_kernel,
        out_shape=[jax.ShapeDtypeStruct((t, n), F32)] * 3,
        grid=(n_seq, nt),
        in_specs=[pl.BlockSpec((tm, c), lambda b, i: (b * nt + i, 0)),
                  pl.BlockSpec((SUBLANES, c), lambda b, i: (jnp.maximum((b * nt + i) * blocks8 - 1, 0), 0)),
                  pl.BlockSpec((1, SUBLANES, c), lambda b, i: (b, 0, 0)),
                  pl.BlockSpec(w.shape, lambda b, i: (0, 0))],
        out_specs=[out, out, out],
        compiler_params=_params("arbitrary", "arbitrary"),
        name="conv",
    )(x, x, hist8, w)


def _split(x, terms):
    parts = []
    for _ in range(terms):
        p = x.astype(BF16)
        parts.append(p)
        x = x - p.astype(F32)
    return parts


def _dot_left01(a01, b):
    return sum(_dot(a01, p) for p in _split(b, 3))


def _dot_split(lefts, b):
    b_hi, b_lo = _split(b, 2)
    out = []
    for a in lefts:
        a_hi, a_lo = _split(a, 2)
        out.append(_dot(a_hi, b_hi) + _dot(a_lo, b_hi) + _dot(a_hi, b_lo))
    return out


def _delta_kernel(q_ref, k_ref, v_ref, ab_ref, z_ref, s0_ref, alog_ref, dtb_ref, gain_ref,
                  o_ref, s_ref, u_ref, w_ref, qk_ref, qe_ref, kd_ref, decay_last_ref,
                  *, chunk, n_chunks, valid_rows):
    c = chunk

    @pl.when(pl.program_id(1) == 0)
    def _():
        s_ref[...] = s0_ref[...]

    r = DN_HEADS * c
    ri = lax.broadcasted_iota(jnp.int32, (r, r), 0)
    ci = lax.broadcasted_iota(jnp.int32, (r, r), 1)
    same_head = (ri // c) == (ci // c)
    incl = same_head & (ri >= ci)
    strict = same_head & (ri > ci)
    eye = ri == ci
    ones = jnp.ones((SUBLANES, r), BF16)
    ltri =(lax.broadcasted_iota(jnp.int32, (c, c), 0) >= lax.broadcasted_iota(jnp.int32, (c, c), 1)).astype(BF16)
    neg_a = -jnp.exp(alog_ref[...])
    dtb = dtb_ref[...]
    gain = gain_ref[...]
    n_doublings = max(c.bit_length() - 1, 0)
    heads = range(DN_HEADS)

    def head_cols(h):
        return slice(h * DN_DIM, (h + 1) * DN_DIM)

    def stacked_rows(ch):
        return pl.ds(pl.multiple_of(ch * r, r), r)

    def local(ch):
        rows = pl.ds(pl.multiple_of(ch * c, c), c)
        gb = ab_ref[rows, :]
        g_all = neg_a * _softplus(gb + dtb)
        beta_all = _sigmoid(gb)
        if valid_rows < c:
            live = lax.broadcasted_iota(jnp.int32, g_all.shape, 0) < valid_rows
            g_all = jnp.where(live, g_all, 0.0)
            beta_all = jnp.where(live, beta_all, 0.0)
        gcum_all = _dot_left01(ltri, g_all)
        yield
        decay_last_ref[pl.ds(pl.multiple_of(ch * SUBLANES, SUBLANES), SUBLANES), :] = jnp.broadcast_to(
            jnp.exp(gcum_all[c - 1:c, :]), (SUBLANES, LANES))
        beta = jnp.concatenate([beta_all[:, DN_HEADS + h:DN_HEADS + h + 1] for h in heads], axis=0)
        gcum = jnp.concatenate([gcum_all[:, h:h + 1] for h in heads], axis=0)
        g_last = jnp.concatenate([jnp.broadcast_to(gcum_all[c - 1:c, h:h + 1], (c, 1)) for h in heads], axis=0)
        gcum_cols = jnp.broadcast_to(gcum, (r, r))
        gcum_row = _dot_left01(ones, jnp.where(eye, gcum_cols, 0.0))[0:1, :]
        yield
        decay = jnp.where(incl, jnp.exp(jnp.minimum(gcum_cols - gcum_row, 0.0)), 0.0)
        q = jnp.concatenate([q_ref[rows, head_cols(h)] for h in heads], axis=0)
        k = jnp.concatenate([k_ref[rows, head_cols(h)] for h in heads], axis=0)
        v = jnp.concatenate([v_ref[rows, head_cols(h)] for h in heads], axis=0)
        kb = k * beta
        with_k = _dot_nt(jnp.concatenate([kb, q], axis=0).astype(BF16), k.astype(BF16))
        yield
        m = jnp.where(strict, with_k[:r] * decay, 0.0)
        out = stacked_rows(ch)
        qk_ref[out, :] = (with_k[r:] * decay).astype(BF16)
        inv = jnp.where(eye, 1.0, 0.0) - m
        pw, = _dot_split([m], m)
        yield
        for step in range(1, n_doublings):
            if step + 1 < n_doublings:
                grown, pw = _dot_split([inv, pw], pw)
            else:
                grown, = _dot_split([inv], pw)
            inv = inv + grown
            yield
        e_gcum = jnp.exp(gcum)
        sol, = _dot_split([inv], jnp.concatenate([v * beta, kb * e_gcum], axis=1))
        u_ref[out, :] = sol[:, :DN_DIM]
        w_ref[out, :] = sol[:, DN_DIM:].astype(BF16)
        qe_ref[out, :] = (q * e_gcum).astype(BF16)
        kd_ref[out, :] = (k * jnp.exp(g_last - gcum)).astype(BF16)

    group = math.gcd(n_chunks, LOCAL_GROUP)

    def local_group(i, carry):
        _run_staged([local(group * i + g) for g in range(group)])
        return carry

    lax.fori_loop(0, n_chunks // group, local_group, 0)

    def scan(ch, carry):
        rows = pl.ds(pl.multiple_of(ch * c, c), c)
        decay_last = decay_last_ref[pl.ds(pl.multiple_of(ch * SUBLANES, SUBLANES), SUBLANES), :]
        s = [s_ref[0, h] for h in heads]
        s16 = [x.astype(BF16) for x in s]
        head_rows = [pl.ds(pl.multiple_of(ch * r + h * c, c), c) for h in heads]
        u16 = [(u_ref[head_rows[h], :] - _dot(w_ref[head_rows[h], :], s16[h])).astype(BF16) for h in heads]
        u16_all = jnp.concatenate(u16, axis=0)
        for h in heads:
            o = _dot(qe_ref[head_rows[h], :], s16[h]) + _dot(qk_ref[head_rows[h], :], u16_all)
            s_ref[0, h] = s[h] * decay_last[0:1, h:h + 1] + _dot_tn(kd_ref[head_rows[h], :], u16[h])
            zz = z_ref[rows, head_cols(h)]
            o_ref[rows, head_cols(h)] = _rms(o, gain) * (zz * _sigmoid(zz))
        return carry

    lax.fori_loop(0, n_chunks, scan, 0)


def _delta(q, k, v, ab, z, s0, a_log, dt_bias, gain, n_seq, chunk, n_chunks, valid_rows):
    t, n = q.shape
    rows = chunk * n_chunks
    stacked = DN_HEADS * rows
    steps = t // n_seq // rows
    blk = lambda w: pl.BlockSpec((rows, w), lambda b, i: (b * steps + i, 0))
    state = pl.BlockSpec((1,) + s0.shape[1:], lambda b, i: (b, 0, 0, 0))
    vec = pl.BlockSpec((1, LANES), lambda b, i: (0, 0))
    return pl.pallas_call(
        functools.partial(_delta_kernel, chunk=chunk, n_chunks=n_chunks, valid_rows=valid_rows),
        out_shape=[jax.ShapeDtypeStruct((t, n), F32), jax.ShapeDtypeStruct(s0.shape, F32)],
        grid=(n_seq, steps),
        in_specs=[blk(n), blk(n), blk(n), blk(LANES), blk(n), state, vec, vec, vec],
        out_specs=[blk(n), state],
        scratch_shapes=[pltpu.VMEM((stacked, DN_DIM), F32), pltpu.VMEM((stacked, DN_DIM), BF16),
                        pltpu.VMEM((stacked, DN_HEADS * chunk), BF16),
                        pltpu.VMEM((stacked, DN_DIM), BF16), pltpu.VMEM((stacked, DN_DIM), BF16),
                        pltpu.VMEM((n_chunks * SUBLANES, LANES), F32)],
        compiler_params=_params("arbitrary", "arbitrary"),
        name="delta",
    )(q, k, v, ab, z, s0, a_log, dt_bias, gain)


def _softplus2(x):
    return jnp.maximum(x, 0.0) + jnp.log(1.0 + jnp.exp2(jnp.minimum(x, -x))) * LOG2E


def _sb_weights(z2, later, usum, valid=None):
    n = usum.shape[0]
    sp = _softplus2(z2)
    if valid is not None:
        sp = jnp.where(valid, sp, 0.0)
    pieces = []
    for b in reversed(range(z2.shape[1] // n)):
        cols = slice(b * n, (b + 1) * n)
        within = _dot(sp[:, cols].astype(BF16), usum)
        pieces.append(jnp.exp2(z2[:, cols] - within - later))
        later = later + jnp.sum(sp[:, cols], axis=-1, keepdims=True)
    a = pieces[0] if len(pieces) == 1 else jnp.concatenate(pieces[::-1], axis=1)
    if valid is not None:
        a = jnp.where(valid, a, 0.0)
    return a.astype(BF16), later


def _suffix_sum_matrix(n):
    j = lax.broadcasted_iota(jnp.int32, (n, n), 0)
    s = lax.broadcasted_iota(jnp.int32, (n, n), 1)
    return (j >= s).astype(BF16)


def _sbp_kernel(bias_ref, q_ref, k_ref, v_ref, gain_ref, usum_ref, bd_ref, o_ref, *, tq, sub):
    pair = pl.program_id(1)
    i = pl.program_id(2)
    n_sub = tq // sub
    items = [(hh, s) for hh in range(2) for s in range(n_sub)]
    first = lax.broadcasted_iota(jnp.int32, (sub, LANES), 1) < SB_DIM

    def queries(hh, s):
        q = q_ref[0, s * sub:(s + 1) * sub, :] * (SB_DIM ** -0.5 * LOG2E)
        return (jnp.where(first, q, 0.0) if hh == 0 else jnp.where(first, 0.0, q)).astype(BF16)

    q_items = [queries(hh, s) for hh, s in items]
    bias = (bias_ref[2 * pair] * LOG2E, bias_ref[2 * pair + 1] * LOG2E)
    usum = usum_ref[...]

    def block(j, carry, diagonal):
        start = pl.multiple_of(j * tq, tq)

        def keys_of(ref, s):
            return ref[0, pl.ds(start, (s + 1) * sub if diagonal else tq), :]

        def causal(s):
            key = lax.broadcasted_iota(jnp.int32, (sub, (s + 1) * sub), 1)
            return key < lax.broadcasted_iota(jnp.int32, (sub, (s + 1) * sub), 0) + s * sub

        state = list(carry)
        logits, weights = {}, {}
        for step in range(len(items) + 2):
            if step < len(items):
                hh, s = items[step]
                logits[step] = _dot_nt(q_items[step], keys_of(k_ref, s)) + bias[hh]
            if 0 <= step - 1 < len(items):
                n = step - 1
                later, acc = state[n]
                weights[n], later = _sb_weights(logits.pop(n), later, usum,
                                                causal(items[n][1]) if diagonal else None)
                state[n] = (later, acc)
            if 0 <= step - 2 < len(items):
                n = step - 2
                later, acc = state[n]
                state[n] = (later, acc + _dot(weights.pop(n), keys_of(v_ref, items[n][1])))
        return tuple(state)

    zero = (jnp.zeros((sub, 1), F32), jnp.zeros((sub, LANES), F32))
    carry = block(i, (zero,) * len(items), True)
    carry = lax.fori_loop(0, i, lambda t, c: block(i - 1 - t, c, False), carry)
    for s in range(n_sub):
        o = jnp.where(first, carry[s][1], carry[n_sub + s][1])
        ss = _dot(o * o, bd_ref[...], HIGHEST)
        o_ref[0, s * sub:(s + 1) * sub, :] = o * lax.rsqrt(ss * (1.0 / SB_DIM) + EPS) * gain_ref[...]


def _sb_prompt(q, k16, v16, bias, gain2, n_seq, tq):
    t, n = q.shape
    l = t // n_seq
    pairs = n // LANES
    q3 = q.reshape(n_seq, l, n)
    k3 = k16.reshape(n_seq, l, n)
    v3 = v16.reshape(n_seq, l, n)
    seg = jnp.arange(LANES) // SB_DIM
    bd = (seg[:, None] == seg[None, :]).astype(F32)
    usum = _suffix_sum_matrix(SUFFIX_SUM_WIDTH)
    kv = pl.BlockSpec((1, l, LANES), lambda b, p, i: (b, 0, p))
    const = lambda b, p, i: (0, 0)
    out = pl.pallas_call(
        functools.partial(_sbp_kernel, tq=tq, sub=tq // 2),
        out_shape=jax.ShapeDtypeStruct((n_seq, l, n), F32),
        grid=(n_seq, pairs, l // tq),
        in_specs=[pl.BlockSpec(memory_space=pltpu.SMEM),
                  pl.BlockSpec((1, tq, LANES), lambda b, p, i: (b, i, p)),
                  kv, kv,
                  pl.BlockSpec((1, LANES), const),
                  pl.BlockSpec(usum.shape, const),
                  pl.BlockSpec(bd.shape, const)],
        out_specs=pl.BlockSpec((1, tq, LANES), lambda b, p, i: (b, i, p)),
        compiler_params=_params("arbitrary", "arbitrary", "arbitrary"),
        name="sb_prompt",
    )(bias, q3, k3, v3, gain2, usum, bd)
    return out.reshape(t, n)


def _sbd_kernel(pt_ref, q_ref, bias_ref, knew_ref, vnew_ref, gain_ref, usum_ref, *rest, pages, n_new):
    k_pages = rest[:pages]
    v_pages = rest[pages:2 * pages]
    o_ref = rest[2 * pages]
    later_ref, acc_ref, kpad_ref, vpad_ref = rest[2 * pages + 1:]
    step = pl.program_id(1)
    n_rows, width = acc_ref.shape
    page = kpad_ref.shape[0]
    own_head = (lax.broadcasted_iota(jnp.int32, (n_rows, width), 1) // SB_DIM
                == lax.broadcasted_iota(jnp.int32, (n_rows, width), 0) % SB_HEADS)
    q = jnp.where(own_head, q_ref[0] * (SB_DIM ** -0.5 * LOG2E), 0.0).astype(BF16)
    bias = bias_ref[...] * LOG2E
    usum = usum_ref[...]

    def block(z2, weighted_sum, valid):
        a, later = _sb_weights(z2 + bias, later_ref[...], usum, valid)
        later_ref[...] = later
        acc_ref[...] += weighted_sum(a)

    @pl.when(step == 0)
    def _():
        later_ref[...] = jnp.zeros_like(later_ref)
        acc_ref[...] = jnp.zeros_like(acc_ref)
        kpad_ref[...] = jnp.zeros_like(kpad_ref)
        vpad_ref[...] = jnp.zeros_like(vpad_ref)
        kpad_ref[0:n_new, :] = knew_ref[0]
        vpad_ref[0:n_new, :] = vnew_ref[0]
        key = lax.broadcasted_iota(jnp.int32, (n_rows, page), 1)
        tok = lax.broadcasted_iota(jnp.int32, (n_rows, page), 0) // SB_HEADS
        vs = vpad_ref[...].astype(BF16)
        block(_dot_nt(q, kpad_ref[...].astype(BF16)), lambda a: _dot(a, vs), key < tok)

    def weighted_sum(a):
        return sum(_dot_nt(a[:, i * page:(i + 1) * page], v_pages[i][0, 0].astype(BF16)) for i in range(pages))

    block(jnp.concatenate([_dot(q, k_pages[i][0, 0].astype(BF16)) for i in range(pages)], axis=1),
          weighted_sum, None)

    @pl.when(step == pl.num_programs(1) - 1)
    def _():
        o = jnp.where(own_head, acc_ref[...], 0.0)
        ss = jnp.sum(o * o, axis=-1, keepdims=True)
        o = o * lax.rsqrt(ss * (1.0 / SB_DIM) + EPS)
        o = jnp.sum(o.reshape(n_rows // SB_HEADS, SB_HEADS, width), axis=1)
        o_ref[0] = o * gain_ref[...]


def _sb_decode(q, k_new, v_new, cache_k, cache_v, page_table, layer, bias, gain, pages):
    n_seq, n_new, n = q.shape
    page = cache_k.shape[3]
    n_pages = page_table.shape[1]
    steps = n_pages // pages
    n_rows = n_new * SB_HEADS
    q_rep = jnp.repeat(q, SB_HEADS, axis=1)
    bias_col = jnp.tile(bias, n_new).reshape(n_rows, 1)
    gain_row = jnp.tile(gain, SB_HEADS).reshape(1, n)
    usum = _suffix_sum_matrix(page)

    def page_spec(i):
        return pl.BlockSpec((1, 1, n, page),
                            lambda b, s, pt: (pt[b, (steps - 1 - s) * pages + i], layer, 0, 0))

    per_seq = lambda r: pl.BlockSpec((1, r, n), lambda b, s, pt: (b, 0, 0))
    const = lambda b, s, pt: (0, 0)
    grid_spec = pltpu.PrefetchScalarGridSpec(
        num_scalar_prefetch=1,
        grid=(n_seq, steps),
        in_specs=[per_seq(n_rows),
                  pl.BlockSpec(bias_col.shape, const),
                  per_seq(n_new), per_seq(n_new),
                  pl.BlockSpec(gain_row.shape, const),
                  pl.BlockSpec(usum.shape, const)]
                 + [page_spec(i) for i in range(pages)] * 2,
        out_specs=per_seq(n_new),
        scratch_shapes=[pltpu.VMEM((n_rows, 1), F32), pltpu.VMEM((n_rows, n), F32),
                        pltpu.VMEM((page, n), F32), pltpu.VMEM((page, n), F32)],
    )
    return pl.pallas_call(
        functools.partial(_sbd_kernel, pages=pages, n_new=n_new),
        out_shape=jax.ShapeDtypeStruct((n_seq, n_new, n), F32),
        grid_spec=grid_spec,
        compiler_params=_params("arbitrary", "arbitrary"),
        name="sb_decode",
    )(page_table, q_rep, bias_col, k_new, v_new, gain_row, usum,
      *([cache_k] * pages), *([cache_v] * pages))


def _pad_lanes(v):
    return jnp.pad(v.astype(F32), (0, LANES - v.shape[0])).reshape(1, LANES)


def _run_group(x, weights, *, tm, past):
    n_seq, l, d = x.shape
    depth = weights["norms"].shape[0]
    nq = DN_HEADS * DN_DIM
    sb = SB_HEADS * SB_DIM
    h = x.reshape(n_seq * l, d)
    ks, vs, ss, cs = [], [], [], []
    kv_rows = (None, None)
    for layer in range(depth):
        norms = weights["norms"][layer]
        w_in = weights["w_in"][layer].astype(BF16)
        w_qkv = w_in[:, :3 * nq]
        w_z = w_in[:, 3 * nq:4 * nq]
        w_ab = jnp.pad(w_in[:, 4 * nq:4 * nq + 2 * DN_HEADS], ((0, 0), (0, LANES - 2 * DN_HEADS)))
        w_sb = w_in[:, 4 * nq + 2 * DN_HEADS:]
        w_o = weights["w_o"][layer].astype(BF16)
        a_log = _pad_lanes(weights["a_log"][layer])
        dt_bias = _pad_lanes(weights["dt_bias"][layer])
        dn_gain = weights["dn_out_norm"][layer].reshape(1, DN_DIM)
        sb_bias = weights["sb_logit_bias"][layer]
        sb_gain = weights["sb_out_norm"][layer]

        h = _ffn(h, norms, weights["ffn1_w_gate_up"][layer].astype(BF16),
                 weights["ffn1_w_down"][layer].astype(BF16), 0, 1, tm)
        conv_w = weights["conv_w"][layer]
        if past is None:
            q_dn, k_dn, v_dn, tail, z, ab, q_sb, k16, v16, *kv_rows = _inproj(
                h, norms, w_qkv, w_z, w_ab, w_sb, n_seq, tm, fresh=(conv_w, *kv_rows))
            s0 = jnp.zeros((n_seq, DN_HEADS, DN_DIM, DN_DIM), F32)
            o_dn, s_new = _delta(q_dn, k_dn, v_dn, ab, z, s0, a_log, dt_bias, dn_gain,
                                 n_seq, chunk=64, n_chunks=8, valid_rows=64)
            o_sb = _sb_prompt(q_sb, k16, v16, sb_bias, jnp.tile(sb_gain, 2).reshape(1, LANES), n_seq, tq=512)
            conv_new = tail[:, SUBLANES - (CONV_W - 1):]
        else:
            conv_state, delta_state, cache_k, cache_v, page_table = past
            qkv, z, ab, q_sb, k16, v16, k_sb, v_sb = _inproj(h, norms, w_qkv, w_z, w_ab, w_sb, 1, tm)
            qkv3 = qkv.reshape(n_seq, l, 3 * nq)
            hist = conv_state[layer]
            tile = jnp.concatenate(
                [hist, qkv3, jnp.zeros((n_seq, SUBLANES - l - hist.shape[1], 3 * nq), F32)], axis=1)
            tile = tile.reshape(n_seq * SUBLANES, 3 * nq)
            conv_out = _conv(tile, jnp.zeros((n_seq, SUBLANES, 3 * nq), F32), conv_w, n_seq, SUBLANES)
            first = hist.shape[1]

            def one_chunk(a):
                return jnp.pad(a, ((0, 0), (0, SAMPLE_CHUNK - l), (0, 0))).reshape(n_seq * SAMPLE_CHUNK, -1)

            q_dn, k_dn, v_dn = (one_chunk(a.reshape(n_seq, SUBLANES, nq)[:, first:first + l]) for a in conv_out)
            o_dn, s_new = _delta(q_dn, k_dn, v_dn, one_chunk(ab.reshape(n_seq, l, -1)),
                                 one_chunk(z.reshape(n_seq, l, -1)), delta_state[layer], a_log, dt_bias,
                                 dn_gain, n_seq, chunk=SAMPLE_CHUNK, n_chunks=1, valid_rows=l)
            o_dn = o_dn.reshape(n_seq, SAMPLE_CHUNK, nq)[:, :l].reshape(n_seq * l, nq)
            o_sb = _sb_decode(q_sb.reshape(n_seq, l, sb), k_sb.reshape(n_seq, l, sb), v_sb.reshape(n_seq, l, sb),
                              cache_k, cache_v, page_table, layer, sb_bias, sb_gain, pages=16)
            o_sb = o_sb.reshape(n_seq * l, sb)
            conv_new = jnp.concatenate([hist, qkv3], axis=1)[:, -(CONV_W - 1):]
        h = _ffn(h, norms, weights["ffn2_w_gate_up"][layer].astype(BF16),
                 weights["ffn2_w_down"][layer].astype(BF16), 4, 5, tm, mix=(o_dn, o_sb, w_o[:nq], w_o[nq:]))
        if past is not None:
            ks.append(k_sb.reshape(n_seq, l, SB_HEADS, SB_DIM))
            vs.append(v_sb.reshape(n_seq, l, SB_HEADS, SB_DIM))
        ss.append(s_new)
        cs.append(conv_new)
    if past is None:
        k_rows, v_rows = (jnp.transpose(a.reshape(n_seq, depth, SB_HEADS, SB_DIM, l), (0, 4, 1, 2, 3))
                          for a in kv_rows)
    else:
        k_rows, v_rows = jnp.stack(ks, axis=2), jnp.stack(vs, axis=2)
    return h.reshape(n_seq, l, d), k_rows, v_rows, jnp.stack(ss, axis=0), jnp.stack(cs, axis=0)


def kernel(x_prompt, x_sample, cache_sb_k, cache_sb_v, page_table, state_delta, state_conv, norms,
           ffn1_w_gate_up, ffn1_w_down, w_in, conv_w, a_log, dt_bias, dn_out_norm, sb_logit_bias, sb_out_norm,
           w_o, ffn2_w_gate_up, ffn2_w_down):
    weights = dict(norms=norms, ffn1_w_gate_up=ffn1_w_gate_up, ffn1_w_down=ffn1_w_down, w_in=w_in,
                   conv_w=conv_w, a_log=a_log, dt_bias=dt_bias, dn_out_norm=dn_out_norm,
                   sb_logit_bias=sb_logit_bias, sb_out_norm=sb_out_norm, w_o=w_o,
                   ffn2_w_gate_up=ffn2_w_gate_up, ffn2_w_down=ffn2_w_down)
    n_phys, page, depth = cache_sb_k.shape[:3]
    cache_k = jnp.transpose(cache_sb_k, (0, 2, 3, 4, 1)).reshape(n_phys, depth, -1, page)
    cache_v = jnp.transpose(cache_sb_v, (0, 2, 3, 4, 1)).reshape(n_phys, depth, -1, page)
    y_p, k_p, v_p, d_p, c_p = _run_group(x_prompt, weights, tm=512, past=None)
    n_tok = x_sample.shape[0] * x_sample.shape[1]
    y_s, k_s, v_s, d_s, c_s = _run_group(x_sample, weights, tm=n_tok,
                                         past=(state_conv, state_delta, cache_k, cache_v, page_table))
    return (y_p, y_s, k_p, v_p, k_s, v_s, d_p, d_s, c_p, c_s)
```

```python
import functools
import math

import jax
import jax.numpy as jnp
from jax import lax
from jax.experimental import pallas as pl
from jax.experimental.pallas import tpu as pltpu

F32 = jnp.float32
BF16 = jnp.bfloat16
HIGHEST = lax.Precision.HIGHEST

EPS = 1e-6
DN_HEADS = 4
DN_DIM = 128
SB_HEADS = 8
SB_DIM = 64
CONV_W = 4
LANES = 128
SUBLANES = 8
VMEM_LIMIT = 56 * 1024 * 1024
MXU_DIM = 256
SUFFIX_SUM_WIDTH = MXU_DIM
SB_QUERY_ROWS = 1024
SB_ITEM_ROWS = 512
DECODE_PAGES = 32
ROW_TILE = 512
FFN_CHUNK = 256
DN_CHUNK = 64
DN_CHUNKS_PER_STEP = 8
LOG2E = 1.4426950408889634
LOCAL_GROUP = 4
SAMPLE_CHUNK = 16


def _dot(a, b, precision=None):
    return jnp.dot(a, b, preferred_element_type=F32, precision=precision)


def _dot_nt(a, b):
    return lax.dot_general(a, b, (((1,), (1,)), ((), ())), preferred_element_type=F32)


def _dot_tn(a, b):
    return lax.dot_general(a, b, (((0,), (0,)), ((), ())), preferred_element_type=F32)


def _rms(x, gain):
    ms = jnp.mean(x * x, axis=-1, keepdims=True)
    return x * lax.rsqrt(ms + EPS) * gain


def _sigmoid(x):
    return 1.0 / (1.0 + jnp.exp(-x))


def _softplus(x):
    return jnp.maximum(x, 0.0) + jnp.log(1.0 + jnp.exp(-jnp.abs(x)))


def _run_staged(stagers):
    results = [None] * len(stagers)
    live = list(enumerate(stagers))
    while live:
        unfinished = []
        for index, stager in live:
            try:
                next(stager)
                unfinished.append((index, stager))
            except StopIteration as stop:
                results[index] = stop.value
        live = unfinished
    return results


def _params(*sem):
    return pltpu.CompilerParams(dimension_semantics=sem, vmem_limit_bytes=VMEM_LIMIT)


def _ffn_kernel(*refs, pre, post, d_ff, chunk, mixed):
    if mixed:
        a_ref, b_ref, wa_ref, wb_ref, x_ref, g_ref, wgu_ref, wd_ref, o_ref, acc_ref = refs
        mix = _dot(a_ref[...].astype(BF16), wa_ref[...]) + _dot(b_ref[...].astype(BF16), wb_ref[...])
        x = x_ref[...] + _rms(mix, g_ref[pre - 1:pre, :])
    else:
        x_ref, g_ref, wgu_ref, wd_ref, o_ref, acc_ref = refs
        x = x_ref[...]
    xn = _rms(x, g_ref[pre:pre + 1, :]).astype(BF16)
    for c in range(d_ff // chunk):
        gate = _dot(xn, wgu_ref[:, c * chunk:(c + 1) * chunk])
        up = _dot(xn, wgu_ref[:, d_ff + c * chunk:d_ff + (c + 1) * chunk])
        h = (gate * _sigmoid(gate) * up).astype(BF16)
        part = _dot(h, wd_ref[c * chunk:(c + 1) * chunk, :])
        if c == 0:
            acc_ref[...] = part
        else:
            acc_ref[...] += part
    o_ref[...] = x + 0.5 * _rms(acc_ref[...], g_ref[post:post + 1, :])


def _ffn(x, norms, w_gu, w_d, pre, post, tm, mix=None):
    t, d = x.shape
    d_ff = w_d.shape[0]
    const = lambda i: (0, 0)
    row = lambda n: pl.BlockSpec((tm, n), lambda i: (i, 0))
    whole = lambda w: pl.BlockSpec(w.shape, const, pipeline_mode=pl.Buffered(1))
    extra = ()
    extra_specs = []
    if mix is not None:
        a, b, w_a, w_b = mix
        extra = (a, b, w_a, w_b)
        extra_specs = [row(a.shape[1]), row(b.shape[1]), whole(w_a), whole(w_b)]
    return pl.pallas_call(
        functools.partial(_ffn_kernel, pre=pre, post=post, d_ff=d_ff, chunk=FFN_CHUNK, mixed=mix is not None),
        out_shape=jax.ShapeDtypeStruct((t, d), F32),
        grid=(t // tm,),
        in_specs=extra_specs + [row(d), whole(norms), whole(w_gu), whole(w_d)],
        out_specs=row(d),
        scratch_shapes=[pltpu.VMEM((tm, d), F32)],
        compiler_params=_params("arbitrary"),
        name="ffn",
    )(*extra, x, norms, w_gu, w_d)


def _inproj_kernel(*refs, sb, n_prev, feature_major):
    h_ref, g_ref, wqkv_ref, wz_ref, wab_ref, wsb_ref = refs[:6]
    prev = refs[6:8] if n_prev else ()
    qkv_ref, z_ref, ab_ref, q_ref, kb_ref, vb_ref, k_ref, v_ref = refs[6 + len(prev):]
    u = _rms(h_ref[...], g_ref[2:3, :]).astype(BF16)
    qkv_ref[...] = _dot(u, wqkv_ref[...])
    z_ref[...] = _dot(u, wz_ref[...])
    ab_ref[...] = _dot(u, wab_ref[...])
    p = _dot(u, wsb_ref[...])
    k = p[:, sb:2 * sb]
    v = p[:, 2 * sb:]
    q_ref[...] = p[:, :sb]
    kb_ref[...] = k.astype(BF16)
    vb_ref[...] = v.astype(BF16)
    if feature_major:
        for new, old in zip((k_ref, v_ref), prev):
            new[0, 0:n_prev] = old[0]
        k_ref[0, n_prev] = k.T
        v_ref[0, n_prev] = v.T
    else:
        k_ref[...] = k
        v_ref[...] = v


def _inproj(h, norms, w_qkv, w_z, w_ab, w_sb, n_seq, tm, prev_rows):
    t, d = h.shape
    sb = SB_HEADS * SB_DIM
    nt = t // n_seq // tm
    const = lambda b, i: (0, 0)
    row = lambda n: pl.BlockSpec((tm, n), lambda b, i: (b * nt + i, 0))
    widths = (w_qkv.shape[1], w_z.shape[1], LANES, sb, sb, sb)
    dtypes = (F32, F32, F32, F32, BF16, BF16)
    out_shape = [jax.ShapeDtypeStruct((t, n), dt) for n, dt in zip(widths, dtypes)]
    out_specs = [row(n) for n in widths]
    prev = ()
    n_prev = 0
    if prev_rows is None:
        out_shape += [jax.ShapeDtypeStruct((t, sb), F32)] * 2
        out_specs += [row(sb)] * 2
    else:
        if prev_rows[0] is not None:
            prev = tuple(prev_rows)
            n_prev = prev[0].shape[1]
        layers = lambda n: pl.BlockSpec((1, n, sb, tm), lambda b, i: (b, 0, 0, i))
        out_shape += [jax.ShapeDtypeStruct((n_seq, n_prev + 1, sb, t // n_seq), F32)] * 2
        out_specs += [layers(n_prev + 1)] * 2
    return pl.pallas_call(
        functools.partial(_inproj_kernel, sb=sb, n_prev=n_prev, feature_major=prev_rows is not None),
        out_shape=out_shape,
        grid=(n_seq, nt),
        in_specs=[row(d), pl.BlockSpec(norms.shape, const)]
                 + [pl.BlockSpec(w.shape, const) for w in (w_qkv, w_z, w_ab, w_sb)]
                 + [layers(n_prev) for _ in prev],
        out_specs=out_specs,
        compiler_params=_params("arbitrary", "arbitrary"),
        name="inproj",
    )(h, norms, w_qkv, w_z, w_ab, w_sb, *prev)


def _conv_kernel(x_ref, prev_ref, hist_ref, w_ref, q_ref, k_ref, v_ref):
    x = x_ref[...]
    w = w_ref[...]
    prev = jnp.where(pl.program_id(1) == 0, hist_ref[0], prev_ref[...])
    row = lax.broadcasted_iota(jnp.int32, prev.shape, 0)
    y = x * w[CONV_W - 1:CONV_W, :]
    y_head = y[0:SUBLANES]
    for s in range(1, CONV_W):
        tap = w[CONV_W - 1 - s:CONV_W - s, :]
        xs = pltpu.roll(x, s, 0)
        y = y + xs * tap
        head = jnp.where(row < s, pltpu.roll(prev, s, 0), xs[0:SUBLANES])
        y_head = y_head + head * tap

    def finish(y, rows):
        y = y * _sigmoid(y)
        nq = DN_HEADS * DN_DIM
        for h in range(DN_HEADS):
            for ref, off, scale in ((q_ref, 0, DN_DIM ** -0.5), (k_ref, nq, 1.0)):
                t = y[:, off + h * DN_DIM:off + (h + 1) * DN_DIM]
                ss = jnp.sum(t * t, axis=-1, keepdims=True)
                ref[rows, h * DN_DIM:(h + 1) * DN_DIM] = t * (lax.rsqrt(ss + EPS) * scale)
        v_ref[rows, :] = y[:, 2 * nq:]

    finish(y, slice(None))
    finish(y_head, slice(0, SUBLANES))


def _conv(x, hist8, w, n_seq, tm):
    t, c = x.shape
    nt = t // n_seq // tm
    blocks8 = tm // SUBLANES
    n = DN_HEADS * DN_DIM
    out = pl.BlockSpec((tm, n), lambda b, i: (b * nt + i, 0))
    return pl.pallas_call(
        _conv_kernel,
        out_shape=[jax.ShapeDtypeStruct((t, n), F32)] * 3,
        grid=(n_seq, nt),
        in_specs=[pl.BlockSpec((tm, c), lambda b, i: (b * nt + i, 0)),
                  pl.BlockSpec((SUBLANES, c), lambda b, i: (jnp.maximum((b * nt + i) * blocks8 - 1, 0), 0)),
                  pl.BlockSpec((1, SUBLANES, c), lambda b, i: (b, 0, 0)),
                  pl.BlockSpec(w.shape, lambda b, i: (0, 0))],
        out_specs=[out, out, out],
        compiler_params=_params("arbitrary", "arbitrary"),
        name="conv",
    )(x, x, hist8, w)


def _split(x, terms):
    parts = []
    for _ in range(terms):
        p = x.astype(BF16)
        parts.append(p)
        x = x - p.astype(F32)
    return parts


def _dot_left01(a01, b):
    return sum(_dot(a01, p) for p in _split(b, 3))


def _dot_split(lefts, b):
    b_hi, b_lo = _split(b, 2)
    out = []
    for a in lefts:
        a_hi, a_lo = _split(a, 2)
        out.append(_dot(a_hi, b_hi) + _dot(a_lo, b_hi) + _dot(a_hi, b_lo))
    return out


def _delta_kernel(q_ref, k_ref, v_ref, ab_ref, z_ref, s0_ref, alog_ref, dtb_ref, gain_ref,
                  o_ref, s_ref, u_ref, w_ref, qk_ref, qe_ref, kd_ref, decay_last_ref,
                  *, chunk, n_chunks, valid_rows):
    c = chunk

    @pl.when(pl.program_id(1) == 0)
    def _():
        s_ref[...] = s0_ref[...]

    r = DN_HEADS * c
    ri = lax.broadcasted_iota(jnp.int32, (r, r), 0)
    ci = lax.broadcasted_iota(jnp.int32, (r, r), 1)
    same_head = (ri // c) == (ci // c)
    incl = same_head & (ri >= ci)
    strict = same_head & (ri > ci)
    eye = ri == ci
    ones = jnp.ones((SUBLANES, r), BF16)
    ltri = (lax.broadcasted_iota(jnp.int32, (c, c), 0) >= lax.broadcasted_iota(jnp.int32, (c, c), 1)).astype(BF16)
    neg_a = -jnp.exp(alog_ref[...])
    dtb = dtb_ref[...]
    gain = gain_ref[...]
    n_doublings = max(c.bit_length() - 1, 0)
    heads = range(DN_HEADS)

    def head_cols(h):
        return slice(h * DN_DIM, (h + 1) * DN_DIM)

    def stacked_rows(ch):
        return pl.ds(pl.multiple_of(ch * r, r), r)

    def local(ch):
        rows = pl.ds(pl.multiple_of(ch * c, c), c)
        gb = ab_ref[rows, :]
        g_all = neg_a * _softplus(gb + dtb)
        beta_all = _sigmoid(gb)
        if valid_rows < c:
            live = lax.broadcasted_iota(jnp.int32, g_all.shape, 0) < valid_rows
            g_all = jnp.where(live, g_all, 0.0)
            beta_all = jnp.where(live, beta_all, 0.0)
        gcum_all = _dot_left01(ltri, g_all)
        yield
        decay_last_ref[pl.ds(pl.multiple_of(ch * SUBLANES, SUBLANES), SUBLANES), :] = jnp.broadcast_to(
            jnp.exp(gcum_all[c - 1:c, :]), (SUBLANES, LANES))
        beta = jnp.concatenate([beta_all[:, DN_HEADS + h:DN_HEADS + h + 1] for h in heads], axis=0)
        gcum = jnp.concatenate([gcum_all[:, h:h + 1] for h in heads], axis=0)
        g_last = jnp.concatenate([jnp.broadcast_to(gcum_all[c - 1:c, h:h + 1], (c, 1)) for h in heads], axis=0)
        gcum_cols = jnp.broadcast_to(gcum, (r, r))
        gcum_row = _dot_left01(ones, jnp.where(eye, gcum_cols, 0.0))[0:1, :]
        yield
        decay = jnp.where(incl, jnp.exp(jnp.minimum(gcum_cols - gcum_row, 0.0)), 0.0)
        q = jnp.concatenate([q_ref[rows, head_cols(h)] for h in heads], axis=0)
        k = jnp.concatenate([k_ref[rows, head_cols(h)] for h in heads], axis=0)
        v = jnp.concatenate([v_ref[rows, head_cols(h)] for h in heads], axis=0)
        kb = k * beta
        with_k = _dot_nt(jnp.concatenate([kb, q], axis=0).astype(BF16), k.astype(BF16))
        yield
        m = jnp.where(strict, with_k[:r] * decay, 0.0)
        out = stacked_rows(ch)
        qk_ref[out, :] = (with_k[r:] * decay).astype(BF16)
        inv = jnp.where(eye, 1.0, 0.0) - m
        pw, = _dot_split([m], m)
        yield
        for step in range(1, n_doublings):
            if step + 1 < n_doublings:
                grown, pw = _dot_split([inv, pw], pw)
            else:
                grown, = _dot_split([inv], pw)
            inv = inv + grown
            yield
        e_gcum = jnp.exp(gcum)
        sol, = _dot_split([inv], jnp.concatenate([v * beta, kb * e_gcum], axis=1))
        u_ref[out, :] = sol[:, :DN_DIM]
        w_ref[out, :] = sol[:, DN_DIM:].astype(BF16)
        qe_ref[out, :] = (q * e_gcum).astype(BF16)
        kd_ref[out, :] = (k * jnp.exp(g_last - gcum)).astype(BF16)

    group = math.gcd(n_chunks, LOCAL_GROUP)

    def local_group(i, carry):
        _run_staged([local(group * i + g) for g in range(group)])
        return carry

    lax.fori_loop(0, n_chunks // group, local_group, 0)

    def scan(ch, carry):
        rows = pl.ds(pl.multiple_of(ch * c, c), c)
        decay_last = decay_last_ref[pl.ds(pl.multiple_of(ch * SUBLANES, SUBLANES), SUBLANES), :]
        s = [s_ref[0, h] for h in heads]
        s16 = [x.astype(BF16) for x in s]
        head_rows = [pl.ds(pl.multiple_of(ch * r + h * c, c), c) for h in heads]
        u16 = [(u_ref[head_rows[h], :] - _dot(w_ref[head_rows[h], :], s16[h])).astype(BF16) for h in heads]
        u16_all = jnp.concatenate(u16, axis=0)
        for h in heads:
            o = _dot(qe_ref[head_rows[h], :], s16[h]) + _dot(qk_ref[head_rows[h], :], u16_all)
            s_ref[0, h] = s[h] * decay_last[0:1, h:h + 1] + _dot_tn(kd_ref[head_rows[h], :], u16[h])
            zz = z_ref[rows, head_cols(h)]
            o_ref[rows, head_cols(h)] = _rms(o, gain) * (zz * _sigmoid(zz))
        return carry

    lax.fori_loop(0, n_chunks, scan, 0)


def _delta(q, k, v, ab, z, s0, a_log, dt_bias, gain, n_seq, chunk, n_chunks, valid_rows):
    t, n = q.shape
    rows = chunk * n_chunks
    stacked = DN_HEADS * rows
    steps = t // n_seq // rows
    blk = lambda w: pl.BlockSpec((rows, w), lambda b, i: (b * steps + i, 0))
    state = pl.BlockSpec((1,) + s0.shape[1:], lambda b, i: (b, 0, 0, 0))
    vec = pl.BlockSpec((1, LANES), lambda b, i: (0, 0))
    return pl.pallas_call(
        functools.partial(_delta_kernel, chunk=chunk, n_chunks=n_chunks, valid_rows=valid_rows),
        out_shape=[jax.ShapeDtypeStruct((t, n), F32), jax.ShapeDtypeStruct(s0.shape, F32)],
        grid=(n_seq, steps),
        in_specs=[blk(n), blk(n), blk(n), blk(LANES), blk(n), state, vec, vec, vec],
        out_specs=[blk(n), state],
        scratch_shapes=[pltpu.VMEM((stacked, DN_DIM), F32), pltpu.VMEM((stacked, DN_DIM), BF16),
                        pltpu.VMEM((stacked, DN_HEADS * chunk), BF16),
                        pltpu.VMEM((stacked, DN_DIM), BF16), pltpu.VMEM((stacked, DN_DIM), BF16),
                        pltpu.VMEM((n_chunks * SUBLANES, LANES), F32)],
        compiler_params=_params("arbitrary", "arbitrary"),
        name="delta",
    )(q, k, v, ab, z, s0, a_log, dt_bias, gain)


def _softplus2(x):
    return jnp.maximum(x, 0.0) + jnp.log(1.0 + jnp.exp2(jnp.minimum(x, -x))) * LOG2E


def _sb_weights(z2, later, usum, valid=None):
    n = usum.shape[0]
    sp = _softplus2(z2)
    if valid is not None:
        sp = jnp.where(valid, sp, 0.0)
    pieces = []
    for b in reversed(range(z2.shape[1] // n)):
        cols = slice(b * n, (b + 1) * n)
        within = _dot(sp[:, cols].astype(BF16), usum)
        pieces.append(jnp.exp2(z2[:, cols] - within - later))
        later = later + jnp.sum(sp[:, cols], axis=-1, keepdims=True)
    a = pieces[0] if len(pieces) == 1 else jnp.concatenate(pieces[::-1], axis=1)
    if valid is not None:
        a = jnp.where(valid, a, 0.0)
    return a.astype(BF16), later


def _suffix_sum_matrix(n):
    j = lax.broadcasted_iota(jnp.int32, (n, n), 0)
    s = lax.broadcasted_iota(jnp.int32, (n, n), 1)
    return (j >= s).astype(BF16)


def _sbp_kernel(bias_ref, q_ref, k_ref, v_ref, gain_ref, usum_ref, bd_ref, o_ref, *, tq, sub):
    pair = pl.program_id(1)
    i = pl.program_id(2)
    n_sub = tq // sub
    items = [(hh, s) for hh in range(2) for s in range(n_sub)]
    first = lax.broadcasted_iota(jnp.int32, (sub, LANES), 1) < SB_DIM

    def queries(hh, s):
        q = q_ref[0, s * sub:(s + 1) * sub, :] * (SB_DIM ** -0.5 * LOG2E)
        return (jnp.where(first, q, 0.0) if hh == 0 else jnp.where(first, 0.0, q)).astype(BF16)

    q_items = [queries(hh, s) for hh, s in items]
    bias = (bias_ref[2 * pair] * LOG2E, bias_ref[2 * pair + 1] * LOG2E)
    usum = usum_ref[...]

    def block(j, carry, diagonal):
        start = pl.multiple_of(j * tq, tq)

        def keys_of(ref, s):
            return ref[0, pl.ds(start, (s + 1) * sub if diagonal else tq), :]

        def causal(s):
            key = lax.broadcasted_iota(jnp.int32, (sub, (s + 1) * sub), 1)
            return key < lax.broadcasted_iota(jnp.int32, (sub, (s + 1) * sub), 0) + s * sub

        state = list(carry)
        logits, weights = {}, {}
        for step in range(len(items) + 2):
            if step < len(items):
                hh, s = items[step]
                logits[step] = _dot_nt(q_items[step], keys_of(k_ref, s)) + bias[hh]
            if 0 <= step - 1 < len(items):
                n = step - 1
                later, acc = state[n]
                weights[n], later = _sb_weights(logits.pop(n), later, usum,
                                                causal(items[n][1]) if diagonal else None)
                state[n] = (later, acc)
            if 0 <= step - 2 < len(items):
                n = step - 2
                later, acc = state[n]
                state[n] = (later, acc + _dot(weights.pop(n), keys_of(v_ref, items[n][1])))
        return tuple(state)

    zero = (jnp.zeros((sub, 1), F32), jnp.zeros((sub, LANES), F32))
    carry = block(i, (zero,) * len(items), True)
    carry = lax.fori_loop(0, i, lambda t, c: block(i - 1 - t, c, False), carry)
    for s in range(n_sub):
        o = jnp.where(first, carry[s][1], carry[n_sub + s][1])
        ss = _dot(o * o, bd_ref[...], HIGHEST)
        o_ref[0, s * sub:(s + 1) * sub, :] = o * lax.rsqrt(ss * (1.0 / SB_DIM) + EPS) * gain_ref[...]


def _sb_prompt(q, k16, v16, bias, gain2, n_seq, tq):
    t, n = q.shape
    l = t // n_seq
    pairs = n // LANES
    q3 = q.reshape(n_seq, l, n)
    k3 = k16.reshape(n_seq, l, n)
    v3 = v16.reshape(n_seq, l, n)
    seg = jnp.arange(LANES) // SB_DIM
    bd = (seg[:, None] == seg[None, :]).astype(F32)
    usum = _suffix_sum_matrix(SUFFIX_SUM_WIDTH)
    kv = pl.BlockSpec((1, l, LANES), lambda b, p, i: (b, 0, p))
    const = lambda b, p, i: (0, 0)
    out = pl.pallas_call(
        functools.partial(_sbp_kernel, tq=tq, sub=SB_ITEM_ROWS),
        out_shape=jax.ShapeDtypeStruct((n_seq, l, n), F32),
        grid=(n_seq, pairs, l // tq),
        in_specs=[pl.BlockSpec(memory_space=pltpu.SMEM),
                  pl.BlockSpec((1, tq, LANES), lambda b, p, i: (b, i, p)),
                  kv, kv,
                  pl.BlockSpec((1, LANES), const),
                  pl.BlockSpec(usum.shape, const),
                  pl.BlockSpec(bd.shape, const)],
        out_specs=pl.BlockSpec((1, tq, LANES), lambda b, p, i: (b, i, p)),
        compiler_params=_params("arbitrary", "arbitrary", "arbitrary"),
        name="sb_prompt",
    )(bias, q3, k3, v3, gain2, usum, bd)
    return out.reshape(t, n)


def _sbd_kernel(pt_ref, q_ref, bias_ref, knew_ref, vnew_ref, gain_ref, usum_ref, *rest, pages, n_new):
    k_pages = rest[:pages]
    v_pages = rest[pages:2 * pages]
    o_ref = rest[2 * pages]
    later_ref, acc_ref, kpad_ref, vpad_ref = rest[2 * pages + 1:]
    step = pl.program_id(1)
    n_rows, width = acc_ref.shape
    page = kpad_ref.shape[0]
    own_head = (lax.broadcasted_iota(jnp.int32, (n_rows, width), 1) // SB_DIM
                == lax.broadcasted_iota(jnp.int32, (n_rows, width), 0) % SB_HEADS)
    q = jnp.where(own_head, q_ref[0] * (SB_DIM ** -0.5 * LOG2E), 0.0).astype(BF16)
    bias = bias_ref[...] * LOG2E
    usum = usum_ref[...]

    def block(z2, weighted_sum, valid):
        a, later = _sb_weights(z2 + bias, later_ref[...], usum, valid)
        later_ref[...] = later
        acc_ref[...] += weighted_sum(a)

    @pl.when(step == 0)
    def _():
        later_ref[...] = jnp.zeros_like(later_ref)
        acc_ref[...] = jnp.zeros_like(acc_ref)
        kpad_ref[...] = jnp.zeros_like(kpad_ref)
        vpad_ref[...] = jnp.zeros_like(vpad_ref)
        kpad_ref[0:n_new, :] = knew_ref[0]
        vpad_ref[0:n_new, :] = vnew_ref[0]
        key = lax.broadcasted_iota(jnp.int32, (n_rows, page), 1)
        tok = lax.broadcasted_iota(jnp.int32, (n_rows, page), 0) // SB_HEADS
        vs = vpad_ref[...].astype(BF16)
        block(_dot_nt(q, kpad_ref[...].astype(BF16)), lambda a: _dot(a, vs), key < tok)

    def weighted_sum(a):
        return sum(_dot_nt(a[:, i * page:(i + 1) * page], v_pages[i][0, 0].astype(BF16)) for i in range(pages))

    block(jnp.concatenate([_dot(q, k_pages[i][0, 0].astype(BF16)) for i in range(pages)], axis=1),
          weighted_sum, None)

    @pl.when(step == pl.num_programs(1) - 1)
    def _():
        o = jnp.where(own_head, acc_ref[...], 0.0)
        ss = jnp.sum(o * o, axis=-1, keepdims=True)
        o = o * lax.rsqrt(ss * (1.0 / SB_DIM) + EPS)
        o = jnp.sum(o.reshape(n_rows // SB_HEADS, SB_HEADS, width), axis=1)
        o_ref[0] = o * gain_ref[...]


def _sb_decode(q, k_new, v_new, cache_k, cache_v, page_table, layer, bias, gain, pages):
    n_seq, n_new, n = q.shape
    page = cache_k.shape[3]
    n_pages = page_table.shape[1]
    steps = n_pages // pages
    n_rows = n_new * SB_HEADS
    q_rep = jnp.repeat(q, SB_HEADS, axis=1)
    bias_col = jnp.tile(bias, n_new).reshape(n_rows, 1)
    gain_row = jnp.tile(gain, SB_HEADS).reshape(1, n)
    usum = _suffix_sum_matrix(page)

    def page_spec(i):
        return pl.BlockSpec((1, 1, n, page),
                            lambda b, s, pt: (pt[b, (steps - 1 - s) * pages + i], layer, 0, 0))

    per_seq = lambda r: pl.BlockSpec((1, r, n), lambda b, s, pt: (b, 0, 0))
    const = lambda b, s, pt: (0, 0)
    grid_spec = pltpu.PrefetchScalarGridSpec(
        num_scalar_prefetch=1,
        grid=(n_seq, steps),
        in_specs=[per_seq(n_rows),
                  pl.BlockSpec(bias_col.shape, const),
                  per_seq(n_new), per_seq(n_new),
                  pl.BlockSpec(gain_row.shape, const),
                  pl.BlockSpec(usum.shape, const)]
                 + [page_spec(i) for i in range(pages)] * 2,
        out_specs=per_seq(n_new),
        scratch_shapes=[pltpu.VMEM((n_rows, 1), F32), pltpu.VMEM((n_rows, n), F32),
                        pltpu.VMEM((page, n), F32), pltpu.VMEM((page, n), F32)],
    )
    return pl.pallas_call(
        functools.partial(_sbd_kernel, pages=pages, n_new=n_new),
        out_shape=jax.ShapeDtypeStruct((n_seq, n_new, n), F32),
        grid_spec=grid_spec,
        compiler_params=_params("arbitrary", "arbitrary"),
        name="sb_decode",
    )(page_table, q_rep, bias_col, k_new, v_new, gain_row, usum,
      *([cache_k] * pages), *([cache_v] * pages))


def _pad_lanes(v):
    return jnp.pad(v.astype(F32), (0, LANES - v.shape[0])).reshape(1, LANES)


def _run_group(x, weights, *, tm, past):
    n_seq, l, d = x.shape
    depth = weights["norms"].shape[0]
    nq = DN_HEADS * DN_DIM
    sb = SB_HEADS * SB_DIM
    h = x.reshape(n_seq * l, d)
    ks, vs, ss, cs = [], [], [], []
    kv_rows = (None, None)
    for layer in range(depth):
        norms = weights["norms"][layer]
        w_in = weights["w_in"][layer].astype(BF16)
        w_qkv = w_in[:, :3 * nq]
        w_z = w_in[:, 3 * nq:4 * nq]
        w_ab = jnp.pad(w_in[:, 4 * nq:4 * nq + 2 * DN_HEADS], ((0, 0), (0, LANES - 2 * DN_HEADS)))
        w_sb = w_in[:, 4 * nq + 2 * DN_HEADS:]
        w_o = weights["w_o"][layer].astype(BF16)
        a_log = _pad_lanes(weights["a_log"][layer])
        dt_bias = _pad_lanes(weights["dt_bias"][layer])
        dn_gain = weights["dn_out_norm"][layer].reshape(1, DN_DIM)
        sb_bias = weights["sb_logit_bias"][layer]
        sb_gain = weights["sb_out_norm"][layer]

        h = _ffn(h, norms, weights["ffn1_w_gate_up"][layer].astype(BF16),
                 weights["ffn1_w_down"][layer].astype(BF16), 0, 1, tm)
        qkv, z, ab, q_sb, k16, v16, k_sb, v_sb = _inproj(
            h, norms, w_qkv, w_z, w_ab, w_sb, 1 if past else n_seq, tm, None if past else kv_rows)
        conv_w = weights["conv_w"][layer]
        if past is None:
            kv_rows = (k_sb, v_sb)
            hist8 = jnp.zeros((n_seq, SUBLANES, 3 * nq), F32)
            q_dn, k_dn, v_dn = _conv(qkv, hist8, conv_w, n_seq, tm)
            s0 = jnp.zeros((n_seq, DN_HEADS, DN_DIM, DN_DIM), F32)
            o_dn, s_new = _delta(q_dn, k_dn, v_dn, ab, z, s0, a_log, dt_bias, dn_gain,
                                 n_seq, chunk=DN_CHUNK, n_chunks=DN_CHUNKS_PER_STEP, valid_rows=DN_CHUNK)
            o_sb = _sb_prompt(q_sb, k16, v16, sb_bias, jnp.tile(sb_gain, 2).reshape(1, LANES), n_seq,
                              tq=SB_QUERY_ROWS)
            conv_new = qkv.reshape(n_seq, l, 3 * nq)[:, l - (CONV_W - 1):]
        else:
            conv_state, delta_state, cache_k, cache_v, page_table = past
            qkv3 = qkv.reshape(n_seq, l, 3 * nq)
            hist = conv_state[layer]
            tile = jnp.concatenate(
                [hist, qkv3, jnp.zeros((n_seq, SUBLANES - l - hist.shape[1], 3 * nq), F32)], axis=1)
            tile = tile.reshape(n_seq * SUBLANES, 3 * nq)
            conv_out = _conv(tile, jnp.zeros((n_seq, SUBLANES, 3 * nq), F32), conv_w, n_seq, SUBLANES)
            first = hist.shape[1]

            def one_chunk(a):
                return jnp.pad(a, ((0, 0), (0, SAMPLE_CHUNK - l), (0, 0))).reshape(n_seq * SAMPLE_CHUNK, -1)

            q_dn, k_dn, v_dn = (one_chunk(a.reshape(n_seq, SUBLANES, nq)[:, first:first + l]) for a in conv_out)
            o_dn, s_new = _delta(q_dn, k_dn, v_dn, one_chunk(ab.reshape(n_seq, l, -1)),
                                 one_chunk(z.reshape(n_seq, l, -1)), delta_state[layer], a_log, dt_bias,
                                 dn_gain, n_seq, chunk=SAMPLE_CHUNK, n_chunks=1, valid_rows=l)
            o_dn = o_dn.reshape(n_seq, SAMPLE_CHUNK, nq)[:, :l].reshape(n_seq * l, nq)
            o_sb = _sb_decode(q_sb.reshape(n_seq, l, sb), k_sb.reshape(n_seq, l, sb), v_sb.reshape(n_seq, l, sb),
                              cache_k, cache_v, page_table, layer, sb_bias, sb_gain, pages=DECODE_PAGES)
            o_sb = o_sb.reshape(n_seq * l, sb)
            conv_new = jnp.concatenate([hist, qkv3], axis=1)[:, -(CONV_W - 1):]
        h = _ffn(h, norms, weights["ffn2_w_gate_up"][layer].astype(BF16),
                 weights["ffn2_w_down"][layer].astype(BF16), 4, 5, tm, mix=(o_dn, o_sb, w_o[:nq], w_o[nq:]))
        if past is not None:
            ks.append(k_sb.reshape(n_seq, l, SB_HEADS, SB_DIM))
            vs.append(v_sb.reshape(n_seq, l, SB_HEADS, SB_DIM))
        ss.append(s_new)
        cs.append(conv_new)
    if past is None:
        k_rows, v_rows = (jnp.transpose(a.reshape(n_seq, depth, SB_HEADS, SB_DIM, l), (0, 4, 1, 2, 3))
                          for a in kv_rows)
    else:
        k_rows, v_rows = jnp.stack(ks, axis=2), jnp.stack(vs, axis=2)
    return h.reshape(n_seq, l, d), k_rows, v_rows, jnp.stack(ss, axis=0), jnp.stack(cs, axis=0)


def kernel(x_prompt, x_sample, cache_sb_k, cache_sb_v, page_table, state_delta, state_conv, norms,
           ffn1_w_gate_up, ffn1_w_down, w_in, conv_w, a_log, dt_bias, dn_out_norm, sb_logit_bias, sb_out_norm,
           w_o, ffn2_w_gate_up, ffn2_w_down):
    weights = dict(norms=norms, ffn1_w_gate_up=ffn1_w_gate_up, ffn1_w_down=ffn1_w_down, w_in=w_in,
                   conv_w=conv_w, a_log=a_log, dt_bias=dt_bias, dn_out_norm=dn_out_norm,
                   sb_logit_bias=sb_logit_bias, sb_out_norm=sb_out_norm, w_o=w_o,
                   ffn2_w_gate_up=ffn2_w_gate_up, ffn2_w_down=ffn2_w_down)
    n_phys, page, depth = cache_sb_k.shape[:3]
    cache_k = jnp.transpose(cache_sb_k, (0, 2, 3, 4, 1)).reshape(n_phys, depth, -1, page)
    cache_v = jnp.transpose(cache_sb_v, (0, 2, 3, 4, 1)).reshape(n_phys, depth, -1, page)
    y_p, k_p, v_p, d_p, c_p = _run_group(x_prompt, weights, tm=ROW_TILE, past=None)
    n_tok = x_sample.shape[0] * x_sample.shape[1]
    y_s, k_s, v_s, d_s, c_s = _run_group(x_sample, weights, tm=n_tok,
                                         past=(state_conv, state_delta, cache_k, cache_v, page_table))
    return (y_p, y_s, k_p, v_p, k_s, v_s, d_p, d_s, c_p, c_s)
```

```python
import functools
import math

import jax
import jax.numpy as jnp
from jax import lax
from jax.experimental import pallas as pl
from jax.experimental.pallas import tpu as pltpu

F32 = jnp.float32
BF16 = jnp.bfloat16
HIGHEST = lax.Precision.HIGHEST

EPS = 1e-6
DN_HEADS = 4
DN_DIM = 128
SB_HEADS = 8
SB_DIM = 64
CONV_W = 4
LANES = 128
SUBLANES = 8
VMEM_LIMIT = 56 * 1024 * 1024
MXU_DIM = 256
SUFFIX_SUM_WIDTH = MXU_DIM
SB_QUERY_ROWS = 1024
SB_ITEM_ROWS = 512
DECODE_PAGES = 32
ROW_TILE = 512
FFN_CHUNK = 256
DN_CHUNK = 64
DN_CHUNKS_PER_STEP = 8
LOG2E = 1.4426950408889634
LOCAL_GROUP = 4
SAMPLE_CHUNK = 16


def _dot(a, b, precision=None):
    return jnp.dot(a, b, preferred_element_type=F32, precision=precision)


def _dot_nt(a, b):
    return lax.dot_general(a, b, (((1,), (1,)), ((), ())), preferred_element_type=F32)


def _dot_tn(a, b):
    return lax.dot_general(a, b, (((0,), (0,)), ((), ())), preferred_element_type=F32)


def _rms(x, gain):
    ms = jnp.mean(x * x, axis=-1, keepdims=True)
    return x * lax.rsqrt(ms + EPS) * gain


def _sigmoid(x):
    return 1.0 / (1.0 + jnp.exp2(x * -LOG2E))


def _softplus(x):
    return jnp.maximum(x, 0.0) + jnp.log(1.0 + jnp.exp(-jnp.abs(x)))


def _run_staged(stagers):
    results = [None] * len(stagers)
    live = list(enumerate(stagers))
    while live:
        unfinished = []
        for index, stager in live:
            try:
                next(stager)
                unfinished.append((index, stager))
            except StopIteration as stop:
                results[index] = stop.value
        live = unfinished
    return results


def _params(*sem):
    return pltpu.CompilerParams(dimension_semantics=sem, vmem_limit_bytes=VMEM_LIMIT)


def _ffn_kernel(*refs, pre, post, d_ff, chunk, mixed):
    if mixed:
        a_ref, b_ref, wa_ref, wb_ref, x_ref, g_ref, wgu_ref, wd_ref, o_ref, acc_ref = refs
        mix = _dot(a_ref[...].astype(BF16), wa_ref[...]) + _dot(b_ref[...].astype(BF16), wb_ref[...])
        x = x_ref[...] + _rms(mix, g_ref[pre - 1:pre, :])
    else:
        x_ref, g_ref, wgu_ref, wd_ref, o_ref, acc_ref = refs
        x = x_ref[...]
    xn = _rms(x, g_ref[pre:pre + 1, :]).astype(BF16)
    for c in range(d_ff // chunk):
        gate = _dot(xn, wgu_ref[:, c * chunk:(c + 1) * chunk])
        up = _dot(xn, wgu_ref[:, d_ff + c * chunk:d_ff + (c + 1) * chunk])
        h = (gate * _sigmoid(gate) * up).astype(BF16)
        part = _dot(h, wd_ref[c * chunk:(c + 1) * chunk, :])
        if c == 0:
            acc_ref[...] = part
        else:
            acc_ref[...] += part
    o_ref[...] = x + 0.5 * _rms(acc_ref[...], g_ref[post:post + 1, :])


def _ffn(x, norms, w_gu, w_d, pre, post, tm, mix=None):
    t, d = x.shape
    d_ff = w_d.shape[0]
    const = lambda i: (0, 0)
    row = lambda n: pl.BlockSpec((tm, n), lambda i: (i, 0))
    whole = lambda w: pl.BlockSpec(w.shape, const, pipeline_mode=pl.Buffered(1))
    extra = ()
    extra_specs = []
    if mix is not None:
        a, b, w_a, w_b = mix
        extra = (a, b, w_a, w_b)
        extra_specs = [row(a.shape[1]), row(b.shape[1]), whole(w_a), whole(w_b)]
    return pl.pallas_call(
        functools.partial(_ffn_kernel, pre=pre, post=post, d_ff=d_ff, chunk=FFN_CHUNK, mixed=mix is not None),
        out_shape=jax.ShapeDtypeStruct((t, d), F32),
        grid=(t // tm,),
        in_specs=extra_specs + [row(d), whole(norms), whole(w_gu), whole(w_d)],
        out_specs=row(d),
        scratch_shapes=[pltpu.VMEM((tm, d), F32)],
        compiler_params=_params("arbitrary"),
        name="ffn",
    )(*extra, x, norms, w_gu, w_d)


def _inproj_kernel(*refs, sb, n_prev, feature_major):
    h_ref, g_ref, wqkv_ref, wz_ref, wab_ref, wsb_ref = refs[:6]
    prev = refs[6:8] if n_prev else ()
    qkv_ref, z_ref, ab_ref, q_ref, kb_ref, vb_ref, k_ref, v_ref = refs[6 + len(prev):]
    u = _rms(h_ref[...], g_ref[2:3, :]).astype(BF16)
    qkv_ref[...] = _dot(u, wqkv_ref[...])
    z_ref[...] = _dot(u, wz_ref[...])
    ab_ref[...] = _dot(u, wab_ref[...])
    p = _dot(u, wsb_ref[...])
    k = p[:, sb:2 * sb]
    v = p[:, 2 * sb:]
    q_ref[...] = p[:, :sb]
    kb_ref[...] = k.astype(BF16)
    vb_ref[...] = v.astype(BF16)
    if feature_major:
        for new, old in zip((k_ref, v_ref), prev):
            new[0, 0:n_prev] = old[0]
        k_ref[0, n_prev] = k.T
        v_ref[0, n_prev] = v.T
    else:
        k_ref[...] = k
        v_ref[...] = v


def _inproj(h, norms, w_qkv, w_z, w_ab, w_sb, n_seq, tm, prev_rows):
    t, d = h.shape
    sb = SB_HEADS * SB_DIM
    nt = t // n_seq // tm
    const = lambda b, i: (0, 0)
    row = lambda n: pl.BlockSpec((tm, n), lambda b, i: (b * nt + i, 0))
    widths = (w_qkv.shape[1], w_z.shape[1], LANES, sb, sb, sb)
    dtypes = (F32, F32, F32, F32, BF16, BF16)
    out_shape = [jax.ShapeDtypeStruct((t, n), dt) for n, dt in zip(widths, dtypes)]
    out_specs = [row(n) for n in widths]
    prev = ()
    n_prev = 0
    if prev_rows is None:
        out_shape += [jax.ShapeDtypeStruct((t, sb), F32)] * 2
        out_specs += [row(sb)] * 2
    else:
        if prev_rows[0] is not None:
            prev = tuple(prev_rows)
            n_prev = prev[0].shape[1]
        layers = lambda n: pl.BlockSpec((1, n, sb, tm), lambda b, i: (b, 0, 0, i))
        out_shape += [jax.ShapeDtypeStruct((n_seq, n_prev + 1, sb, t // n_seq), F32)] * 2
        out_specs += [layers(n_prev + 1)] * 2
    return pl.pallas_call(
        functools.partial(_inproj_kernel, sb=sb, n_prev=n_prev, feature_major=prev_rows is not None),
        out_shape=out_shape,
        grid=(n_seq, nt),
        in_specs=[row(d), pl.BlockSpec(norms.shape, const)]
                 + [pl.BlockSpec(w.shape, const) for w in (w_qkv, w_z, w_ab, w_sb)]
                 + [layers(n_prev) for _ in prev],
        out_specs=out_specs,
        compiler_params=_params("arbitrary", "arbitrary"),
        name="inproj",
    )(h, norms, w_qkv, w_z, w_ab, w_sb, *prev)


def _conv_kernel(x_ref, prev_ref, hist_ref, w_ref, q_ref, k_ref, v_ref):
    x = x_ref[...]
    w = w_ref[...]
    prev = jnp.where(pl.program_id(1) == 0, hist_ref[0], prev_ref[...])
    row = lax.broadcasted_iota(jnp.int32, prev.shape, 0)
    y = x * w[CONV_W - 1:CONV_W, :]
    y_head = y[0:SUBLANES]
    for s in range(1, CONV_W):
        tap = w[CONV_W - 1 - s:CONV_W - s, :]
        xs = pltpu.roll(x, s, 0)
        y = y + xs * tap
        head = jnp.where(row < s, pltpu.roll(prev, s, 0), xs[0:SUBLANES])
        y_head = y_head + head * tap

    def finish(y, rows):
        y = y * _sigmoid(y)
        nq = DN_HEADS * DN_DIM
        for h in range(DN_HEADS):
            for ref, off, scale in ((q_ref, 0, DN_DIM ** -0.5), (k_ref, nq, 1.0)):
                t = y[:, off + h * DN_DIM:off + (h + 1) * DN_DIM]
                ss = jnp.sum(t * t, axis=-1, keepdims=True)
                ref[rows, h * DN_DIM:(h + 1) * DN_DIM] = t * (lax.rsqrt(ss + EPS) * scale)
        v_ref[rows, :] = y[:, 2 * nq:]

    finish(y, slice(None))
    finish(y_head, slice(0, SUBLANES))


def _conv(x, hist8, w, n_seq, tm):
    t, c = x.shape
    nt = t // n_seq // tm
    blocks8 = tm // SUBLANES
    n = DN_HEADS * DN_DIM
    out = pl.BlockSpec((tm, n), lambda b, i: (b * nt + i, 0))
    return pl.pallas_call(
        _conv_kernel,
        out_shape=[jax.ShapeDtypeStruct((t, n), F32)] * 3,
        grid=(n_seq, nt),
        in_specs=[pl.BlockSpec((tm, c), lambda b, i: (b * nt + i, 0)),
                  pl.BlockSpec((SUBLANES, c), lambda b, i: (jnp.maximum((b * nt + i) * blocks8 - 1, 0), 0)),
                  pl.BlockSpec((1, SUBLANES, c), lambda b, i: (b, 0, 0)),
                  pl.BlockSpec(w.shape, lambda b, i: (0, 0))],
        out_specs=[out, out, out],
        compiler_params=_params("arbitrary", "arbitrary"),
        name="conv",
    )(x, x, hist8, w)


def _split(x, terms):
    parts = []
    for _ in range(terms):
        p = x.astype(BF16)
        parts.append(p)
        x = x - p.astype(F32)
    return parts


def _dot_left01(a01, b):
    return sum(_dot(a01, p) for p in _split(b, 3))


def _dot_split(lefts, b):
    b_hi, b_lo = _split(b, 2)
    out = []
    for a in lefts:
        a_hi, a_lo = _split(a, 2)
        out.append(_dot(a_hi, b_hi) + _dot(a_lo, b_hi) + _dot(a_hi, b_lo))
    return out


def _delta_kernel(q_ref, k_ref, v_ref, ab_ref, z_ref, s0_ref, alog_ref, dtb_ref, gain_ref,
                  o_ref, s_ref, u_ref, w_ref, qk_ref, qe_ref, kd_ref, decay_last_ref,
                  *, chunk, n_chunks, valid_rows, one_sequence):
    c = chunk

    @pl.when(pl.program_id(1) == 0)
    def _():
        s_ref[...] = s0_ref[...]

    r = DN_HEADS * c
    ri = lax.broadcasted_iota(jnp.int32, (r, r), 0)
    ci = lax.broadcasted_iota(jnp.int32, (r, r), 1)
    same_head = (ri // c) == (ci // c)
    incl = same_head & (ri >= ci)
    strict = same_head & (ri > ci)
    eye = ri == ci
    ones = jnp.ones((SUBLANES, r), BF16)
    ltri = (lax.broadcasted_iota(jnp.int32, (c, c), 0) >= lax.broadcasted_iota(jnp.int32, (c, c), 1)).astype(BF16)
    neg_a = -jnp.exp(alog_ref[...])
    dtb = dtb_ref[...]
    gain = gain_ref[...]
    n_doublings = max(c.bit_length() - 1, 0)
    heads = range(DN_HEADS)

    def head_cols(h):
        return slice(h * DN_DIM, (h + 1) * DN_DIM)

    def stacked_rows(ch):
        return pl.ds(pl.multiple_of(ch * r, r), r)

    def local(ch):
        rows = pl.ds(pl.multiple_of(ch * c, c), c)
        gb = ab_ref[rows, :]
        g_all = neg_a * _softplus(gb + dtb)
        beta_all = _sigmoid(gb)
        if valid_rows < c:
            live = lax.broadcasted_iota(jnp.int32, g_all.shape, 0) < valid_rows
            g_all = jnp.where(live, g_all, 0.0)
            beta_all = jnp.where(live, beta_all, 0.0)
        gcum_all = _dot_left01(ltri, g_all)
        yield
        decay_last_ref[pl.ds(pl.multiple_of(ch * SUBLANES, SUBLANES), SUBLANES), :] = jnp.broadcast_to(
            jnp.exp(gcum_all[c - 1:c, :]), (SUBLANES, LANES))
        beta = jnp.concatenate([beta_all[:, DN_HEADS + h:DN_HEADS + h + 1] for h in heads], axis=0)
        gcum = jnp.concatenate([gcum_all[:, h:h + 1] for h in heads], axis=0)
        g_last = jnp.concatenate([jnp.broadcast_to(gcum_all[c - 1:c, h:h + 1], (c, 1)) for h in heads], axis=0)
        gcum_cols = jnp.broadcast_to(gcum, (r, r))
        gcum_row = _dot_left01(ones, jnp.where(eye, gcum_cols, 0.0))[0:1, :]
        yield
        decay = jnp.where(incl, jnp.exp(jnp.minimum(gcum_cols - gcum_row, 0.0)), 0.0)
        q = jnp.concatenate([q_ref[rows, head_cols(h)] for h in heads], axis=0)
        k = jnp.concatenate([k_ref[rows, head_cols(h)] for h in heads], axis=0)
        v = jnp.concatenate([v_ref[rows, head_cols(h)] for h in heads], axis=0)
        kb = k * beta
        with_k = _dot_nt(jnp.concatenate([kb, q], axis=0).astype(BF16), k.astype(BF16))
        yield
        m = jnp.where(strict, with_k[:r] * decay, 0.0)
        out = stacked_rows(ch)
        qk_ref[out, :] = (with_k[r:] * decay).astype(BF16)
        inv = jnp.where(eye, 1.0, 0.0) - m
        pw, = _dot_split([m], m)
        yield
        for step in range(1, n_doublings):
            if step + 1 < n_doublings:
                grown, pw = _dot_split([inv, pw], pw)
            else:
                grown, = _dot_split([inv], pw)
            inv = inv + grown
            yield
        e_gcum = jnp.exp(gcum)
        sol, = _dot_split([inv], jnp.concatenate([v * beta, kb * e_gcum], axis=1))
        u_ref[out, :] = sol[:, :DN_DIM]
        w_ref[out, :] = sol[:, DN_DIM:].astype(BF16)
        qe_ref[out, :] = (q * e_gcum).astype(BF16)
        kd_ref[out, :] = (k * jnp.exp(g_last - gcum)).astype(BF16)

    group = math.gcd(n_chunks, LOCAL_GROUP)

    def local_group(i, carry):
        _run_staged([local(group * i + g) for g in range(group)])
        return carry

    lax.fori_loop(0, n_chunks // group, local_group, 0)

    def scan(ch, carry):
        rows = pl.ds(pl.multiple_of(ch * c, c), c)
        decay_last = decay_last_ref[pl.ds(pl.multiple_of(ch * SUBLANES, SUBLANES), SUBLANES), :]
        seq = 0 if one_sequence else ch
        s = [s_ref[seq, h] for h in heads]
        s16 = [x.astype(BF16) for x in s]
        head_rows = [pl.ds(pl.multiple_of(ch * r + h * c, c), c) for h in heads]
        u16 = [(u_ref[head_rows[h], :] - _dot(w_ref[head_rows[h], :], s16[h])).astype(BF16) for h in heads]
        u16_all = jnp.concatenate(u16, axis=0)
        for h in heads:
            o = _dot(qe_ref[head_rows[h], :], s16[h]) + _dot(qk_ref[head_rows[h], :], u16_all)
            s_ref[seq, h] = s[h] * decay_last[0:1, h:h + 1] + _dot_tn(kd_ref[head_rows[h], :], u16[h])
            zz = z_ref[rows, head_cols(h)]
            o_ref[rows, head_cols(h)] = _rms(o, gain) * (zz * _sigmoid(zz))
        return carry

    lax.fori_loop(0, n_chunks, scan, 0)


def _delta(q, k, v, ab, z, s0, a_log, dt_bias, gain, chunk, n_chunks, valid_rows, one_sequence):
    t, n = q.shape
    rows = chunk * n_chunks
    stacked = DN_HEADS * rows
    n_seq = s0.shape[0]
    outer = n_seq if one_sequence else n_seq // n_chunks
    steps = t // outer // rows
    blk = lambda w: pl.BlockSpec((rows, w), lambda b, i: (b * steps + i, 0))
    state = pl.BlockSpec((n_seq // outer,) + s0.shape[1:], lambda b, i: (b, 0, 0, 0))
    vec = pl.BlockSpec((1, LANES), lambda b, i: (0, 0))
    return pl.pallas_call(
        functools.partial(_delta_kernel, chunk=chunk, n_chunks=n_chunks, valid_rows=valid_rows,
                          one_sequence=one_sequence),
        out_shape=[jax.ShapeDtypeStruct((t, n), F32), jax.ShapeDtypeStruct(s0.shape, F32)],
        grid=(outer, steps),
        in_specs=[blk(n), blk(n), blk(n), blk(LANES), blk(n), state, vec, vec, vec],
        out_specs=[blk(n), state],
        scratch_shapes=[pltpu.VMEM((stacked, DN_DIM), F32), pltpu.VMEM((stacked, DN_DIM), BF16),
                        pltpu.VMEM((stacked, DN_HEADS * chunk), BF16),
                        pltpu.VMEM((stacked, DN_DIM), BF16), pltpu.VMEM((stacked, DN_DIM), BF16),
                        pltpu.VMEM((n_chunks * SUBLANES, LANES), F32)],
        compiler_params=_params("arbitrary", "arbitrary"),
        name="delta",
    )(q, k, v, ab, z, s0, a_log, dt_bias, gain)


def _softplus2(x):
    return jnp.maximum(x, 0.0) + jnp.log(1.0 + jnp.exp2(jnp.minimum(x, -x))) * LOG2E


def _sb_weights(z2, later, usum, valid=None):
    n = usum.shape[0]
    sp = _softplus2(z2)
    if valid is not None:
        sp = jnp.where(valid, sp, 0.0)
    pieces = []
    for b in reversed(range(z2.shape[1] // n)):
        cols = slice(b * n, (b + 1) * n)
        within = _dot(sp[:, cols].astype(BF16), usum)
        pieces.append(jnp.exp2(z2[:, cols] - within - later))
        later = later + jnp.sum(sp[:, cols], axis=-1, keepdims=True)
    a = pieces[0] if len(pieces) == 1 else jnp.concatenate(pieces[::-1], axis=1)
    if valid is not None:
        a = jnp.where(valid, a, 0.0)
    return a.astype(BF16), later


def _suffix_sum_matrix(n):
    j = lax.broadcasted_iota(jnp.int32, (n, n), 0)
    s = lax.broadcasted_iota(jnp.int32, (n, n), 1)
    return (j >= s).astype(BF16)


def _sbp_kernel(bias_ref, q_ref, k_ref, v_ref, gain_ref, usum_ref, bd_ref, o_ref, *, tq, sub):
    pair = pl.program_id(1)
    i = pl.program_id(2)
    n_sub = tq // sub
    items = [(hh, s) for hh in range(2) for s in range(n_sub)]
    first = lax.broadcasted_iota(jnp.int32, (sub, LANES), 1) < SB_DIM

    def queries(hh, s):
        q = q_ref[0, s * sub:(s + 1) * sub, :] * (SB_DIM ** -0.5 * LOG2E)
        return (jnp.where(first, q, 0.0) if hh == 0 else jnp.where(first, 0.0, q)).astype(BF16)

    q_items = [queries(hh, s) for hh, s in items]
    bias = (bias_ref[2 * pair] * LOG2E, bias_ref[2 * pair + 1] * LOG2E)
    usum = usum_ref[...]

    def block(j, carry, diagonal):
        start = pl.multiple_of(j * tq, tq)

        def keys_of(ref, s):
            return ref[0, pl.ds(start, (s + 1) * sub if diagonal else tq), :]

        def causal(s):
            key = lax.broadcasted_iota(jnp.int32, (sub, (s + 1) * sub), 1)
            return key < lax.broadcasted_iota(jnp.int32, (sub, (s + 1) * sub), 0) + s * sub

        state = list(carry)
        logits, weights = {}, {}
        for step in range(len(items) + 2):
            if step < len(items):
                hh, s = items[step]
                logits[step] = _dot_nt(q_items[step], keys_of(k_ref, s)) + bias[hh]
            if 0 <= step - 1 < len(items):
                n = step - 1
                later, acc = state[n]
                weights[n], later = _sb_weights(logits.pop(n), later, usum,
                                                causal(items[n][1]) if diagonal else None)
                state[n] = (later, acc)
            if 0 <= step - 2 < len(items):
                n = step - 2
                later, acc = state[n]
                state[n] = (later, acc + _dot(weights.pop(n), keys_of(v_ref, items[n][1])))
        return tuple(state)

    zero = (jnp.zeros((sub, 1), F32), jnp.zeros((sub, LANES), F32))
    carry = block(i, (zero,) * len(items), True)
    carry = lax.fori_loop(0, i, lambda t, c: block(i - 1 - t, c, False), carry)
    for s in range(n_sub):
        o = jnp.where(first, carry[s][1], carry[n_sub + s][1])
        ss = _dot(o * o, bd_ref[...], HIGHEST)
        o_ref[0, s * sub:(s + 1) * sub, :] = o * lax.rsqrt(ss * (1.0 / SB_DIM) + EPS) * gain_ref[...]


def _sb_prompt(q, k16, v16, bias, gain2, n_seq, tq):
    t, n = q.shape
    l = t // n_seq
    pairs = n // LANES
    q3 = q.reshape(n_seq, l, n)
    k3 = k16.reshape(n_seq, l, n)
    v3 = v16.reshape(n_seq, l, n)
    seg = jnp.arange(LANES) // SB_DIM
    bd = (seg[:, None] == seg[None, :]).astype(F32)
    usum = _suffix_sum_matrix(SUFFIX_SUM_WIDTH)
    kv = pl.BlockSpec((1, l, LANES), lambda b, p, i: (b, 0, p))
    const = lambda b, p, i: (0, 0)
    out = pl.pallas_call(
        functools.partial(_sbp_kernel, tq=tq, sub=SB_ITEM_ROWS),
        out_shape=jax.ShapeDtypeStruct((n_seq, l, n), F32),
        grid=(n_seq, pairs, l // tq),
        in_specs=[pl.BlockSpec(memory_space=pltpu.SMEM),
                  pl.BlockSpec((1, tq, LANES), lambda b, p, i: (b, i, p)),
                  kv, kv,
                  pl.BlockSpec((1, LANES), const),
                  pl.BlockSpec(usum.shape, const),
                  pl.BlockSpec(bd.shape, const)],
        out_specs=pl.BlockSpec((1, tq, LANES), lambda b, p, i: (b, i, p)),
        compiler_params=_params("arbitrary", "arbitrary", "arbitrary"),
        name="sb_prompt",
    )(bias, q3, k3, v3, gain2, usum, bd)
    return out.reshape(t, n)


def _sbd_kernel(pt_ref, q_ref, bias_ref, knew_ref, vnew_ref, gain_ref, usum_ref, *rest, pages, n_new):
    k_pages = rest[:pages]
    v_pages = rest[pages:2 * pages]
    o_ref = rest[2 * pages]
    later_ref, acc_ref, kpad_ref, vpad_ref = rest[2 * pages + 1:]
    step = pl.program_id(1)
    n_rows, width = acc_ref.shape
    page = kpad_ref.shape[0]
    own_head = (lax.broadcasted_iota(jnp.int32, (n_rows, width), 1) // SB_DIM
                == lax.broadcasted_iota(jnp.int32, (n_rows, width), 0) % SB_HEADS)
    q = jnp.where(own_head, q_ref[0] * (SB_DIM ** -0.5 * LOG2E), 0.0).astype(BF16)
    bias = bias_ref[...] * LOG2E
    usum = usum_ref[...]

    def block(z2, weighted_sum, valid):
        a, later = _sb_weights(z2 + bias, later_ref[...], usum, valid)
        later_ref[...] = later
        acc_ref[...] += weighted_sum(a)

    @pl.when(step == 0)
    def _():
        later_ref[...] = jnp.zeros_like(later_ref)
        acc_ref[...] = jnp.zeros_like(acc_ref)
        kpad_ref[...] = jnp.zeros_like(kpad_ref)
        vpad_ref[...] = jnp.zeros_like(vpad_ref)
        kpad_ref[0:n_new, :] = knew_ref[0]
        vpad_ref[0:n_new, :] = vnew_ref[0]
        key = lax.broadcasted_iota(jnp.int32, (n_rows, page), 1)
        tok = lax.broadcasted_iota(jnp.int32, (n_rows, page), 0) // SB_HEADS
        vs = vpad_ref[...].astype(BF16)
        block(_dot_nt(q, kpad_ref[...].astype(BF16)), lambda a: _dot(a, vs), key < tok)

    def weighted_sum(a):
        return sum(_dot_nt(a[:, i * page:(i + 1) * page], v_pages[i][0, 0].astype(BF16)) for i in range(pages))

    block(jnp.concatenate([_dot(q, k_pages[i][0, 0].astype(BF16)) for i in range(pages)], axis=1),
          weighted_sum, None)

    @pl.when(step == pl.num_programs(1) - 1)
    def _():
        o = jnp.where(own_head, acc_ref[...], 0.0)
        ss = jnp.sum(o * o, axis=-1, keepdims=True)
        o = o * lax.rsqrt(ss * (1.0 / SB_DIM) + EPS)
        o = jnp.sum(o.reshape(n_rows // SB_HEADS, SB_HEADS, width), axis=1)
        o_ref[0] = o * gain_ref[...]


def _sb_decode(q, k_new, v_new, cache_k, cache_v, page_table, layer, bias, gain, pages):
    n_seq, n_new, n = q.shape
    page = cache_k.shape[3]
    n_pages = page_table.shape[1]
    steps = n_pages // pages
    n_rows = n_new * SB_HEADS
    q_rep = jnp.repeat(q, SB_HEADS, axis=1)
    bias_col = jnp.tile(bias, n_new).reshape(n_rows, 1)
    gain_row = jnp.tile(gain, SB_HEADS).reshape(1, n)
    usum = _suffix_sum_matrix(page)

    def page_spec(i):
        return pl.BlockSpec((1, 1, n, page),
                            lambda b, s, pt: (pt[b, (steps - 1 - s) * pages + i], layer, 0, 0))

    per_seq = lambda r: pl.BlockSpec((1, r, n), lambda b, s, pt: (b, 0, 0))
    const = lambda b, s, pt: (0, 0)
    grid_spec = pltpu.PrefetchScalarGridSpec(
        num_scalar_prefetch=1,
        grid=(n_seq, steps),
        in_specs=[per_seq(n_rows),
                  pl.BlockSpec(bias_col.shape, const),
                  per_seq(n_new), per_seq(n_new),
                  pl.BlockSpec(gain_row.shape, const),
                  pl.BlockSpec(usum.shape, const)]
                 + [page_spec(i) for i in range(pages)] * 2,
        out_specs=per_seq(n_new),
        scratch_shapes=[pltpu.VMEM((n_rows, 1), F32), pltpu.VMEM((n_rows, n), F32),
                        pltpu.VMEM((page, n), F32), pltpu.VMEM((page, n), F32)],
    )
    return pl.pallas_call(
        functools.partial(_sbd_kernel, pages=pages, n_new=n_new),
        out_shape=jax.ShapeDtypeStruct((n_seq, n_new, n), F32),
        grid_spec=grid_spec,
        compiler_params=_params("arbitrary", "arbitrary"),
        name="sb_decode",
    )(page_table, q_rep, bias_col, k_new, v_new, gain_row, usum,
      *([cache_k] * pages), *([cache_v] * pages))


def _pad_lanes(v):
    return jnp.pad(v.astype(F32), (0, LANES - v.shape[0])).reshape(1, LANES)


def _run_group(x, weights, *, tm, past):
    n_seq, l, d = x.shape
    depth = weights["norms"].shape[0]
    nq = DN_HEADS * DN_DIM
    sb = SB_HEADS * SB_DIM
    h = x.reshape(n_seq * l, d)
    ks, vs, ss, cs = [], [], [], []
    kv_rows = (None, None)
    for layer in range(depth):
        norms = weights["norms"][layer]
        w_in = weights["w_in"][layer].astype(BF16)
        w_qkv = w_in[:, :3 * nq]
        w_z = w_in[:, 3 * nq:4 * nq]
        w_ab = jnp.pad(w_in[:, 4 * nq:4 * nq + 2 * DN_HEADS], ((0, 0), (0, LANES - 2 * DN_HEADS)))
        w_sb = w_in[:, 4 * nq + 2 * DN_HEADS:]
        w_o = weights["w_o"][layer].astype(BF16)
        a_log = _pad_lanes(weights["a_log"][layer])
        dt_bias = _pad_lanes(weights["dt_bias"][layer])
        dn_gain = weights["dn_out_norm"][layer].reshape(1, DN_DIM)
        sb_bias = weights["sb_logit_bias"][layer]
        sb_gain = weights["sb_out_norm"][layer]

        h = _ffn(h, norms, weights["ffn1_w_gate_up"][layer].astype(BF16),
                 weights["ffn1_w_down"][layer].astype(BF16), 0, 1, tm)
        qkv, z, ab, q_sb, k16, v16, k_sb, v_sb = _inproj(
            h, norms, w_qkv, w_z, w_ab, w_sb, 1 if past else n_seq, tm, None if past else kv_rows)
        conv_w = weights["conv_w"][layer]
        if past is None:
            kv_rows = (k_sb, v_sb)
            hist8 = jnp.zeros((n_seq, SUBLANES, 3 * nq), F32)
            q_dn, k_dn, v_dn = _conv(qkv, hist8, conv_w, n_seq, tm)
            s0 = jnp.zeros((n_seq, DN_HEADS, DN_DIM, DN_DIM), F32)
            o_dn, s_new = _delta(q_dn, k_dn, v_dn, ab, z, s0, a_log, dt_bias, dn_gain, chunk=DN_CHUNK,
                                 n_chunks=DN_CHUNKS_PER_STEP, valid_rows=DN_CHUNK, one_sequence=True)
            o_sb = _sb_prompt(q_sb, k16, v16, sb_bias, jnp.tile(sb_gain, 2).reshape(1, LANES), n_seq,
                              tq=SB_QUERY_ROWS)
            conv_new = qkv.reshape(n_seq, l, 3 * nq)[:, l - (CONV_W - 1):]
        else:
            conv_state, delta_state, cache_k, cache_v, page_table = past
            qkv3 = qkv.reshape(n_seq, l, 3 * nq)
            hist = conv_state[layer]
            tile = jnp.concatenate(
                [hist, qkv3, jnp.zeros((n_seq, SUBLANES - l - hist.shape[1], 3 * nq), F32)], axis=1)
            tile = tile.reshape(n_seq * SUBLANES, 3 * nq)
            conv_out = _conv(tile, jnp.zeros((1, SUBLANES, 3 * nq), F32), conv_w, 1, n_seq * SUBLANES)
            first = hist.shape[1]

            def one_chunk(a):
                return jnp.pad(a, ((0, 0), (0, SAMPLE_CHUNK - l), (0, 0))).reshape(n_seq * SAMPLE_CHUNK, -1)

            q_dn, k_dn, v_dn = (one_chunk(a.reshape(n_seq, SUBLANES, nq)[:, first:first + l]) for a in conv_out)
            o_dn, s_new = _delta(q_dn, k_dn, v_dn, one_chunk(ab.reshape(n_seq, l, -1)),
                                 one_chunk(z.reshape(n_seq, l, -1)), delta_state[layer], a_log, dt_bias,
                                 dn_gain, chunk=SAMPLE_CHUNK, n_chunks=DN_CHUNKS_PER_STEP, valid_rows=l,
                                 one_sequence=False)
            o_dn = o_dn.reshape(n_seq, SAMPLE_CHUNK, nq)[:, :l].reshape(n_seq * l, nq)
            o_sb = _sb_decode(q_sb.reshape(n_seq, l, sb), k_sb.reshape(n_seq, l, sb), v_sb.reshape(n_seq, l, sb),
                              cache_k, cache_v, page_table, layer, sb_bias, sb_gain, pages=DECODE_PAGES)
            o_sb = o_sb.reshape(n_seq * l, sb)
            conv_new = jnp.concatenate([hist, qkv3], axis=1)[:, -(CONV_W - 1):]
        h = _ffn(h, norms, weights["ffn2_w_gate_up"][layer].astype(BF16),
                 weights["ffn2_w_down"][layer].astype(BF16), 4, 5, tm, mix=(o_dn, o_sb, w_o[:nq], w_o[nq:]))
        if past is not None:
            ks.append(k_sb.reshape(n_seq, l, SB_HEADS, SB_DIM))
            vs.append(v_sb.reshape(n_seq, l, SB_HEADS, SB_DIM))
        ss.append(s_new)
        cs.append(conv_new)
    if past is None:
        k_rows, v_rows = (jnp.transpose(a.reshape(n_seq, depth, SB_HEADS, SB_DIM, l), (0, 4, 1, 2, 3))
                          for a in kv_rows)
    else:
        k_rows, v_rows = jnp.stack(ks, axis=2), jnp.stack(vs, axis=2)
    return h.reshape(n_seq, l, d), k_rows, v_rows, jnp.stack(ss, axis=0), jnp.stack(cs, axis=0)


def kernel(x_prompt, x_sample, cache_sb_k, cache_sb_v, page_table, state_delta, state_conv, norms,
           ffn1_w_gate_up, ffn1_w_down, w_in, conv_w, a_log, dt_bias, dn_out_norm, sb_logit_bias, sb_out_norm,
           w_o, ffn2_w_gate_up, ffn2_w_down):
    weights = dict(norms=norms, ffn1_w_gate_up=ffn1_w_gate_up, ffn1_w_down=ffn1_w_down, w_in=w_in,
                   conv_w=conv_w, a_log=a_log, dt_bias=dt_bias, dn_out_norm=dn_out_norm,
                   sb_logit_bias=sb_logit_bias, sb_out_norm=sb_out_norm, w_o=w_o,
                   ffn2_w_gate_up=ffn2_w_gate_up, ffn2_w_down=ffn2_w_down)
    n_phys, page, depth = cache_sb_k.shape[:3]
    cache_k = jnp.transpose(cache_sb_k, (0, 2, 3, 4, 1)).reshape(n_phys, depth, -1, page)
    cache_v = jnp.transpose(cache_sb_v, (0, 2, 3, 4, 1)).reshape(n_phys, depth, -1, page)
    y_p, k_p, v_p, d_p, c_p = _run_group(x_prompt, weights, tm=ROW_TILE, past=None)
    n_tok = x_sample.shape[0] * x_sample.shape[1]
    y_s, k_s, v_s, d_s, c_s = _run_group(x_sample, weights, tm=n_tok,
                                         past=(state_conv, state_delta, cache_k, cache_v, page_table))
    return (y_p, y_s, k_p, v_p, k_s, v_s, d_p, d_s, c_p, c_s)
```

```python
import functools
import math

import jax
import jax.numpy as jnp
from jax import lax
from jax.experimental import pallas as pl
from jax.experimental.pallas import tpu as pltpu

F32 = jnp.float32
BF16 = jnp.bfloat16
HIGHEST = lax.Precision.HIGHEST

EPS = 1e-6
DN_HEADS = 4
DN_DIM = 128
SB_HEADS = 8
SB_DIM = 64
CONV_W = 4
LANES = 128
SUBLANES = 8
VMEM_LIMIT = 56 * 1024 * 1024
MXU_DIM = 256
SUFFIX_SUM_WIDTH = MXU_DIM
SB_QUERY_ROWS = 1024
SB_ITEM_ROWS = 512
DECODE_PAGES = 32
ROW_TILE = 512
FFN_CHUNK = 256
DN_CHUNK = 64
DN_CHUNKS_PER_STEP = 8
LOG2E = 1.4426950408889634
LOCAL_GROUP = 8
SAMPLE_CHUNK = 16


def _dot(a, b, precision=None):
    return jnp.dot(a, b, preferred_element_type=F32, precision=precision)


def _dot_nt(a, b):
    return lax.dot_general(a, b, (((1,), (1,)), ((), ())), preferred_element_type=F32)


def _dot_tn(a, b):
    return lax.dot_general(a, b, (((0,), (0,)), ((), ())), preferred_element_type=F32)


def _rms(x, gain):
    ms = jnp.mean(x * x, axis=-1, keepdims=True)
    return x * lax.rsqrt(ms + EPS) * gain


def _sigmoid(x):
    return 1.0 / (1.0 + jnp.exp2(x * -LOG2E))


def _softplus(x):
    return jnp.maximum(x, 0.0) + jnp.log(1.0 + jnp.exp(-jnp.abs(x)))


def _run_staged(stagers):
    results = [None] * len(stagers)
    live = list(enumerate(stagers))
    while live:
        unfinished = []
        for index, stager in live:
            try:
                next(stager)
                unfinished.append((index, stager))
            except StopIteration as stop:
                results[index] = stop.value
        live = unfinished
    return results


def _params(*sem):
    return pltpu.CompilerParams(dimension_semantics=sem, vmem_limit_bytes=VMEM_LIMIT)


def _ffn_kernel(*refs, pre, post, d_ff, chunk, mixed):
    if mixed:
        a_ref, b_ref, wa_ref, wb_ref, x_ref, g_ref, wgu_ref, wd_ref, o_ref, acc_ref = refs
        mix = _dot(a_ref[...].astype(BF16), wa_ref[...]) + _dot(b_ref[...].astype(BF16), wb_ref[...])
        x = x_ref[...] + _rms(mix, g_ref[pre - 1:pre, :])
    else:
        x_ref, g_ref, wgu_ref, wd_ref, o_ref, acc_ref = refs
        x = x_ref[...]
    xn = _rms(x, g_ref[pre:pre + 1, :]).astype(BF16)
    for c in range(d_ff // chunk):
        gate = _dot(xn, wgu_ref[:, c * chunk:(c + 1) * chunk])
        up = _dot(xn, wgu_ref[:, d_ff + c * chunk:d_ff + (c + 1) * chunk])
        h = (gate * _sigmoid(gate) * up).astype(BF16)
        part = _dot(h, wd_ref[c * chunk:(c + 1) * chunk, :])
        if c == 0:
            acc_ref[...] = part
        else:
            acc_ref[...] += part
    o_ref[...] = x + 0.5 * _rms(acc_ref[...], g_ref[post:post + 1, :])


def _ffn(x, norms, w_gu, w_d, pre, post, tm, mix=None):
    t, d = x.shape
    d_ff = w_d.shape[0]
    const = lambda i: (0, 0)
    row = lambda n: pl.BlockSpec((tm, n), lambda i: (i, 0))
    whole = lambda w: pl.BlockSpec(w.shape, const, pipeline_mode=pl.Buffered(1))
    extra = ()
    extra_specs = []
    if mix is not None:
        a, b, w_a, w_b = mix
        extra = (a, b, w_a, w_b)
        extra_specs = [row(a.shape[1]), row(b.shape[1]), whole(w_a), whole(w_b)]
    return pl.pallas_call(
        functools.partial(_ffn_kernel, pre=pre, post=post, d_ff=d_ff, chunk=FFN_CHUNK, mixed=mix is not None),
        out_shape=jax.ShapeDtypeStruct((t, d), F32),
        grid=(t // tm,),
        in_specs=extra_specs + [row(d), whole(norms), whole(w_gu), whole(w_d)],
        out_specs=row(d),
        scratch_shapes=[pltpu.VMEM((tm, d), F32)],
        compiler_params=_params("arbitrary"),
        name="ffn",
    )(*extra, x, norms, w_gu, w_d)


def _inproj_kernel(*refs, sb, n_prev, feature_major):
    h_ref, g_ref, wqkv_ref, wz_ref, wab_ref, wsb_ref = refs[:6]
    prev = refs[6:8] if n_prev else ()
    qkv_ref, z_ref, ab_ref, q_ref, kb_ref, vb_ref, k_ref, v_ref = refs[6 + len(prev):]
    u = _rms(h_ref[...], g_ref[2:3, :]).astype(BF16)
    qkv_ref[...] = _dot(u, wqkv_ref[...])
    z_ref[...] = _dot(u, wz_ref[...])
    ab_ref[...] = _dot(u, wab_ref[...])
    p = _dot(u, wsb_ref[...])
    k = p[:, sb:2 * sb]
    v = p[:, 2 * sb:]
    q_ref[...] = p[:, :sb]
    kb_ref[...] = k.astype(BF16)
    vb_ref[...] = v.astype(BF16)
    if feature_major:
        for new, old in zip((k_ref, v_ref), prev):
            new[0, 0:n_prev] = old[0]
        k_ref[0, n_prev] = k.T
        v_ref[0, n_prev] = v.T
    else:
        k_ref[...] = k
        v_ref[...] = v


def _inproj(h, norms, w_qkv, w_z, w_ab, w_sb, n_seq, tm, prev_rows):
    t, d = h.shape
    sb = SB_HEADS * SB_DIM
    nt = t // n_seq // tm
    const = lambda b, i: (0, 0)
    row = lambda n: pl.BlockSpec((tm, n), lambda b, i: (b * nt + i, 0))
    widths = (w_qkv.shape[1], w_z.shape[1], LANES, sb, sb, sb)
    dtypes = (F32, F32, F32, F32, BF16, BF16)
    out_shape = [jax.ShapeDtypeStruct((t, n), dt) for n, dt in zip(widths, dtypes)]
    out_specs = [row(n) for n in widths]
    prev = ()
    n_prev = 0
    if prev_rows is None:
        out_shape += [jax.ShapeDtypeStruct((t, sb), F32)] * 2
        out_specs += [row(sb)] * 2
    else:
        if prev_rows[0] is not None:
            prev = tuple(prev_rows)
            n_prev = prev[0].shape[1]
        layers = lambda n: pl.BlockSpec((1, n, sb, tm), lambda b, i: (b, 0, 0, i))
        out_shape += [jax.ShapeDtypeStruct((n_seq, n_prev + 1, sb, t // n_seq), F32)] * 2
        out_specs += [layers(n_prev + 1)] * 2
    return pl.pallas_call(
        functools.partial(_inproj_kernel, sb=sb, n_prev=n_prev, feature_major=prev_rows is not None),
        out_shape=out_shape,
        grid=(n_seq, nt),
        in_specs=[row(d), pl.BlockSpec(norms.shape, const)]
                 + [pl.BlockSpec(w.shape, const) for w in (w_qkv, w_z, w_ab, w_sb)]
                 + [layers(n_prev) for _ in prev],
        out_specs=out_specs,
        compiler_params=_params("arbitrary", "arbitrary"),
        name="inproj",
    )(h, norms, w_qkv, w_z, w_ab, w_sb, *prev)


def _conv_kernel(x_ref, prev_ref, hist_ref, w_ref, q_ref, k_ref, v_ref):
    x = x_ref[...]
    w = w_ref[...]
    prev = jnp.where(pl.program_id(1) == 0, hist_ref[0], prev_ref[...])
    row = lax.broadcasted_iota(jnp.int32, prev.shape, 0)
    y = x * w[CONV_W - 1:CONV_W, :]
    y_head = y[0:SUBLANES]
    for s in range(1, CONV_W):
        tap = w[CONV_W - 1 - s:CONV_W - s, :]
        xs = pltpu.roll(x, s, 0)
        y = y + xs * tap
        head = jnp.where(row < s, pltpu.roll(prev, s, 0), xs[0:SUBLANES])
        y_head = y_head + head * tap

    def finish(y, rows):
        y = y * _sigmoid(y)
        nq = DN_HEADS * DN_DIM
        for h in range(DN_HEADS):
            for ref, off, scale in ((q_ref, 0, DN_DIM ** -0.5), (k_ref, nq, 1.0)):
                t = y[:, off + h * DN_DIM:off + (h + 1) * DN_DIM]
                ss = jnp.sum(t * t, axis=-1, keepdims=True)
                ref[rows, h * DN_DIM:(h + 1) * DN_DIM] = t * (lax.rsqrt(ss + EPS) * scale)
        v_ref[rows, :] = y[:, 2 * nq:]

    finish(y, slice(None))
    finish(y_head, slice(0, SUBLANES))


def _conv(x, hist8, w, n_seq, tm):
    t, c = x.shape
    nt = t // n_seq // tm
    blocks8 = tm // SUBLANES
    n = DN_HEADS * DN_DIM
    out = pl.BlockSpec((tm, n), lambda b, i: (b * nt + i, 0))
    return pl.pallas_call(
        _conv_kernel,
        out_shape=[jax.ShapeDtypeStruct((t, n), F32)] * 3,
        grid=(n_seq, nt),
        in_specs=[pl.BlockSpec((tm, c), lambda b, i: (b * nt + i, 0)),
                  pl.BlockSpec((SUBLANES, c), lambda b, i: (jnp.maximum((b * nt + i) * blocks8 - 1, 0), 0)),
                  pl.BlockSpec((1, SUBLANES, c), lambda b, i: (b, 0, 0)),
                  pl.BlockSpec(w.shape, lambda b, i: (0, 0))],
        out_specs=[out, out, out],
        compiler_params=_params("arbitrary", "arbitrary"),
        name="conv",
    )(x, x, hist8, w)


def _split(x, terms):
    parts = []
    for _ in range(terms):
        p = x.astype(BF16)
        parts.append(p)
        x = x - p.astype(F32)
    return parts


def _dot_left01(a01, b):
    return sum(_dot(a01, p) for p in _split(b, 3))


def _dot_split(lefts, b):
    b_hi, b_lo = _split(b, 2)
    out = []
    for a in lefts:
        a_hi, a_lo = _split(a, 2)
        out.append(_dot(a_hi, b_hi) + _dot(a_lo, b_hi) + _dot(a_hi, b_lo))
    return out


def _delta_kernel(q_ref, k_ref, v_ref, ab_ref, z_ref, s0_ref, alog_ref, dtb_ref, gain_ref,
                  o_ref, s_ref, u_ref, w_ref, qk_ref, qe_ref, kd_ref, decay_last_ref,
                  *, chunk, n_chunks, valid_rows, one_sequence):
    c = chunk

    @pl.when(pl.program_id(1) == 0)
    def _():
        s_ref[...] = s0_ref[...]

    r = DN_HEADS * c
    ri = lax.broadcasted_iota(jnp.int32, (r, r), 0)
    ci = lax.broadcasted_iota(jnp.int32, (r, r), 1)
    same_head = (ri // c) == (ci // c)
    incl = same_head & (ri >= ci)
    strict = same_head & (ri > ci)
    eye = ri == ci
    ones = jnp.ones((SUBLANES, r), BF16)
    ltri = (lax.broadcasted_iota(jnp.int32, (c, c), 0) >= lax.broadcasted_iota(jnp.int32, (c, c), 1)).astype(BF16)
    neg_a = -jnp.exp(alog_ref[...])
    dtb = dtb_ref[...]
    gain = gain_ref[...]
    n_doublings = max(c.bit_length() - 1, 0)
    heads = range(DN_HEADS)

    def head_cols(h):
        return slice(h * DN_DIM, (h + 1) * DN_DIM)

    def stacked_rows(ch):
        return pl.ds(pl.multiple_of(ch * r, r), r)

    def local(ch):
        rows = pl.ds(pl.multiple_of(ch * c, c), c)
        gb = ab_ref[rows, :]
        g_all = neg_a * _softplus(gb + dtb)
        beta_all = _sigmoid(gb)
        if valid_rows < c:
            live = lax.broadcasted_iota(jnp.int32, g_all.shape, 0) < valid_rows
            g_all = jnp.where(live, g_all, 0.0)
            beta_all = jnp.where(live, beta_all, 0.0)
        gcum_all = _dot_left01(ltri, g_all)
        yield
        decay_last_ref[pl.ds(pl.multiple_of(ch * SUBLANES, SUBLANES), SUBLANES), :] = jnp.broadcast_to(
            jnp.exp(gcum_all[c - 1:c, :]), (SUBLANES, LANES))
        beta = jnp.concatenate([beta_all[:, DN_HEADS + h:DN_HEADS + h + 1] for h in heads], axis=0)
        gcum = jnp.concatenate([gcum_all[:, h:h + 1] for h in heads], axis=0)
        g_last = jnp.concatenate([jnp.broadcast_to(gcum_all[c - 1:c, h:h + 1], (c, 1)) for h in heads], axis=0)
        gcum_cols = jnp.broadcast_to(gcum, (r, r))
        gcum_row = _dot_left01(ones, jnp.where(eye, gcum_cols, 0.0))[0:1, :]
        yield
        decay = jnp.where(incl, jnp.exp(jnp.minimum(gcum_cols - gcum_row, 0.0)), 0.0)
        q = jnp.concatenate([q_ref[rows, head_cols(h)] for h in heads], axis=0)
        k = jnp.concatenate([k_ref[rows, head_cols(h)] for h in heads], axis=0)
        v = jnp.concatenate([v_ref[rows, head_cols(h)] for h in heads], axis=0)
        kb = k * beta
        with_k = _dot_nt(jnp.concatenate([kb, q], axis=0).astype(BF16), k.astype(BF16))
        yield
        m = jnp.where(strict, with_k[:r] * decay, 0.0)
        out = stacked_rows(ch)
        qk_ref[out, :] = (with_k[r:] * decay).astype(BF16)
        inv = jnp.where(eye, 1.0, 0.0) - m
        pw, = _dot_split([m], m)
        yield
        for step in range(1, n_doublings):
            if step + 1 < n_doublings:
                grown, pw = _dot_split([inv, pw], pw)
            else:
                grown, = _dot_split([inv], pw)
            inv = inv + grown
            yield
        e_gcum = jnp.exp(gcum)
        sol, = _dot_split([inv], jnp.concatenate([v * beta, kb * e_gcum], axis=1))
        u_ref[out, :] = sol[:, :DN_DIM]
        w_ref[out, :] = sol[:, DN_DIM:].astype(BF16)
        qe_ref[out, :] = (q * e_gcum).astype(BF16)
        kd_ref[out, :] = (k * jnp.exp(g_last - gcum)).astype(BF16)

    group = math.gcd(n_chunks, LOCAL_GROUP)

    def local_group(i, carry):
        _run_staged([local(group * i + g) for g in range(group)])
        return carry

    lax.fori_loop(0, n_chunks // group, local_group, 0)

    def scan(ch, carry):
        rows = pl.ds(pl.multiple_of(ch * c, c), c)
        decay_last = decay_last_ref[pl.ds(pl.multiple_of(ch * SUBLANES, SUBLANES), SUBLANES), :]
        seq = 0 if one_sequence else ch
        s = [s_ref[seq, h] for h in heads]
        s16 = [x.astype(BF16) for x in s]
        head_rows = [pl.ds(pl.multiple_of(ch * r + h * c, c), c) for h in heads]
        u16 = [(u_ref[head_rows[h], :] - _dot(w_ref[head_rows[h], :], s16[h])).astype(BF16) for h in heads]
        u16_all = jnp.concatenate(u16, axis=0)
        for h in heads:
            o = _dot(qe_ref[head_rows[h], :], s16[h]) + _dot(qk_ref[head_rows[h], :], u16_all)
            s_ref[seq, h] = s[h] * decay_last[0:1, h:h + 1] + _dot_tn(kd_ref[head_rows[h], :], u16[h])
            zz = z_ref[rows, head_cols(h)]
            o_ref[rows, head_cols(h)] = _rms(o, gain) * (zz * _sigmoid(zz))
        return carry

    lax.fori_loop(0, n_chunks, scan, 0)


def _delta(q, k, v, ab, z, s0, a_log, dt_bias, gain, chunk, n_chunks, valid_rows, one_sequence):
    t, n = q.shape
    rows = chunk * n_chunks
    stacked = DN_HEADS * rows
    n_seq = s0.shape[0]
    outer = n_seq if one_sequence else n_seq // n_chunks
    steps = t // outer // rows
    blk = lambda w: pl.BlockSpec((rows, w), lambda b, i: (b * steps + i, 0))
    state = pl.BlockSpec((n_seq // outer,) + s0.shape[1:], lambda b, i: (b, 0, 0, 0))
    vec = pl.BlockSpec((1, LANES), lambda b, i: (0, 0))
    return pl.pallas_call(
        functools.partial(_delta_kernel, chunk=chunk, n_chunks=n_chunks, valid_rows=valid_rows,
                          one_sequence=one_sequence),
        out_shape=[jax.ShapeDtypeStruct((t, n), F32), jax.ShapeDtypeStruct(s0.shape, F32)],
        grid=(outer, steps),
        in_specs=[blk(n), blk(n), blk(n), blk(LANES), blk(n), state, vec, vec, vec],
        out_specs=[blk(n), state],
        scratch_shapes=[pltpu.VMEM((stacked, DN_DIM), F32), pltpu.VMEM((stacked, DN_DIM), BF16),
                        pltpu.VMEM((stacked, DN_HEADS * chunk), BF16),
                        pltpu.VMEM((stacked, DN_DIM), BF16), pltpu.VMEM((stacked, DN_DIM), BF16),
                        pltpu.VMEM((n_chunks * SUBLANES, LANES), F32)],
        compiler_params=_params("arbitrary", "arbitrary"),
        name="delta",
    )(q, k, v, ab, z, s0, a_log, dt_bias, gain)


def _softplus2(x):
    return jnp.maximum(x, 0.0) + jnp.log(1.0 + jnp.exp2(jnp.minimum(x, -x))) * LOG2E


def _sb_weights(z2, later, usum, valid=None):
    n = usum.shape[0]
    sp = _softplus2(z2)
    if valid is not None:
        sp = jnp.where(valid, sp, 0.0)
    pieces = []
    for b in reversed(range(z2.shape[1] // n)):
        cols = slice(b * n, (b + 1) * n)
        within = _dot(sp[:, cols].astype(BF16), usum)
        pieces.append(jnp.exp2(z2[:, cols] - within - later))
        later = later + jnp.sum(sp[:, cols], axis=-1, keepdims=True)
    a = pieces[0] if len(pieces) == 1 else jnp.concatenate(pieces[::-1], axis=1)
    if valid is not None:
        a = jnp.where(valid, a, 0.0)
    return a.astype(BF16), later


def _suffix_sum_matrix(n):
    j = lax.broadcasted_iota(jnp.int32, (n, n), 0)
    s = lax.broadcasted_iota(jnp.int32, (n, n), 1)
    return (j >= s).astype(BF16)


def _sbp_kernel(bias_ref, q_ref, k_ref, v_ref, gain_ref, usum_ref, bd_ref, o_ref, *, tq, sub):
    pair = pl.program_id(1)
    i = pl.program_id(2)
    n_sub = tq // sub
    items = [(hh, s) for hh in range(2) for s in range(n_sub)]
    first = lax.broadcasted_iota(jnp.int32, (sub, LANES), 1) < SB_DIM

    def queries(hh, s):
        q = q_ref[0, s * sub:(s + 1) * sub, :] * (SB_DIM ** -0.5 * LOG2E)
        return (jnp.where(first, q, 0.0) if hh == 0 else jnp.where(first, 0.0, q)).astype(BF16)

    q_items = [queries(hh, s) for hh, s in items]
    bias = (bias_ref[2 * pair] * LOG2E, bias_ref[2 * pair + 1] * LOG2E)
    usum = usum_ref[...]

    def block(j, carry, diagonal):
        start = pl.multiple_of(j * tq, tq)

        def keys_of(ref, s):
            return ref[0, pl.ds(start, (s + 1) * sub if diagonal else tq), :]

        def causal(s):
            key = lax.broadcasted_iota(jnp.int32, (sub, (s + 1) * sub), 1)
            return key < lax.broadcasted_iota(jnp.int32, (sub, (s + 1) * sub), 0) + s * sub

        state = list(carry)
        logits, weights = {}, {}
        for step in range(len(items) + 2):
            if step < len(items):
                hh, s = items[step]
                logits[step] = _dot_nt(q_items[step], keys_of(k_ref, s)) + bias[hh]
            if 0 <= step - 1 < len(items):
                n = step - 1
                later, acc = state[n]
                weights[n], later = _sb_weights(logits.pop(n), later, usum,
                                                causal(items[n][1]) if diagonal else None)
                state[n] = (later, acc)
            if 0 <= step - 2 < len(items):
                n = step - 2
                later, acc = state[n]
                state[n] = (later, acc + _dot(weights.pop(n), keys_of(v_ref, items[n][1])))
        return tuple(state)

    zero = (jnp.zeros((sub, 1), F32), jnp.zeros((sub, LANES), F32))
    carry = block(i, (zero,) * len(items), True)
    carry = lax.fori_loop(0, i, lambda t, c: block(i - 1 - t, c, False), carry)
    for s in range(n_sub):
        o = jnp.where(first, carry[s][1], carry[n_sub + s][1])
        ss = _dot(o * o, bd_ref[...], HIGHEST)
        o_ref[0, s * sub:(s + 1) * sub, :] = o * lax.rsqrt(ss * (1.0 / SB_DIM) + EPS) * gain_ref[...]


def _sb_prompt(q, k16, v16, bias, gain2, n_seq, tq):
    t, n = q.shape
    l = t // n_seq
    pairs = n // LANES
    q3 = q.reshape(n_seq, l, n)
    k3 = k16.reshape(n_seq, l, n)
    v3 = v16.reshape(n_seq, l, n)
    seg = jnp.arange(LANES) // SB_DIM
    bd = (seg[:, None] == seg[None, :]).astype(F32)
    usum = _suffix_sum_matrix(SUFFIX_SUM_WIDTH)
    kv = pl.BlockSpec((1, l, LANES), lambda b, p, i: (b, 0, p))
    const = lambda b, p, i: (0, 0)
    out = pl.pallas_call(
        functools.partial(_sbp_kernel, tq=tq, sub=SB_ITEM_ROWS),
        out_shape=jax.ShapeDtypeStruct((n_seq, l, n), F32),
        grid=(n_seq, pairs, l // tq),
        in_specs=[pl.BlockSpec(memory_space=pltpu.SMEM),
                  pl.BlockSpec((1, tq, LANES), lambda b, p, i: (b, i, p)),
                  kv, kv,
                  pl.BlockSpec((1, LANES), const),
                  pl.BlockSpec(usum.shape, const),
                  pl.BlockSpec(bd.shape, const)],
        out_specs=pl.BlockSpec((1, tq, LANES), lambda b, p, i: (b, i, p)),
        compiler_params=_params("arbitrary", "arbitrary", "arbitrary"),
        name="sb_prompt",
    )(bias, q3, k3, v3, gain2, usum, bd)
    return out.reshape(t, n)


def _sbd_kernel(pt_ref, q_ref, bias_ref, knew_ref, vnew_ref, gain_ref, usum_ref, *rest, pages, n_new):
    k_pages = rest[:pages]
    v_pages = rest[pages:2 * pages]
    o_ref = rest[2 * pages]
    later_ref, acc_ref, kpad_ref, vpad_ref = rest[2 * pages + 1:]
    step = pl.program_id(1)
    n_rows, width = acc_ref.shape
    page = kpad_ref.shape[0]
    own_head = (lax.broadcasted_iota(jnp.int32, (n_rows, width), 1) // SB_DIM
                == lax.broadcasted_iota(jnp.int32, (n_rows, width), 0) % SB_HEADS)
    q = jnp.where(own_head, q_ref[0] * (SB_DIM ** -0.5 * LOG2E), 0.0).astype(BF16)
    bias = bias_ref[...] * LOG2E
    usum = usum_ref[...]

    def block(z2, weighted_sum, valid):
        a, later = _sb_weights(z2 + bias, later_ref[...], usum, valid)
        later_ref[...] = later
        acc_ref[...] += weighted_sum(a)

    @pl.when(step == 0)
    def _():
        later_ref[...] = jnp.zeros_like(later_ref)
        acc_ref[...] = jnp.zeros_like(acc_ref)
        kpad_ref[...] = jnp.zeros_like(kpad_ref)
        vpad_ref[...] = jnp.zeros_like(vpad_ref)
        kpad_ref[0:n_new, :] = knew_ref[0]
        vpad_ref[0:n_new, :] = vnew_ref[0]
        key = lax.broadcasted_iota(jnp.int32, (n_rows, page), 1)
        tok = lax.broadcasted_iota(jnp.int32, (n_rows, page), 0) // SB_HEADS
        vs = vpad_ref[...].astype(BF16)
        block(_dot_nt(q, kpad_ref[...].astype(BF16)), lambda a: _dot(a, vs), key < tok)

    def weighted_sum(a):
        return sum(_dot_nt(a[:, i * page:(i + 1) * page], v_pages[i][0, 0].astype(BF16)) for i in range(pages))

    block(jnp.concatenate([_dot(q, k_pages[i][0, 0].astype(BF16)) for i in range(pages)], axis=1),
          weighted_sum, None)

    @pl.when(step == pl.num_programs(1) - 1)
    def _():
        o = jnp.where(own_head, acc_ref[...], 0.0)
        ss = jnp.sum(o * o, axis=-1, keepdims=True)
        o = o * lax.rsqrt(ss * (1.0 / SB_DIM) + EPS)
        o = jnp.sum(o.reshape(n_rows // SB_HEADS, SB_HEADS, width), axis=1)
        o_ref[0] = o * gain_ref[...]


def _sb_decode(q, k_new, v_new, cache_k, cache_v, page_table, layer, bias, gain, pages):
    n_seq, n_new, n = q.shape
    page = cache_k.shape[3]
    n_pages = page_table.shape[1]
    steps = n_pages // pages
    n_rows = n_new * SB_HEADS
    q_rep = jnp.repeat(q, SB_HEADS, axis=1)
    bias_col = jnp.tile(bias, n_new).reshape(n_rows, 1)
    gain_row = jnp.tile(gain, SB_HEADS).reshape(1, n)
    usum = _suffix_sum_matrix(page)

    def page_spec(i):
        return pl.BlockSpec((1, 1, n, page),
                            lambda b, s, pt: (pt[b, (steps - 1 - s) * pages + i], layer, 0, 0))

    per_seq = lambda r: pl.BlockSpec((1, r, n), lambda b, s, pt: (b, 0, 0))
    const = lambda b, s, pt: (0, 0)
    grid_spec = pltpu.PrefetchScalarGridSpec(
        num_scalar_prefetch=1,
        grid=(n_seq, steps),
        in_specs=[per_seq(n_rows),
                  pl.BlockSpec(bias_col.shape, const),
                  per_seq(n_new), per_seq(n_new),
                  pl.BlockSpec(gain_row.shape, const),
                  pl.BlockSpec(usum.shape, const)]
                 + [page_spec(i) for i in range(pages)] * 2,
        out_specs=per_seq(n_new),
        scratch_shapes=[pltpu.VMEM((n_rows, 1), F32), pltpu.VMEM((n_rows, n), F32),
                        pltpu.VMEM((page, n), F32), pltpu.VMEM((page, n), F32)],
    )
    return pl.pallas_call(
        functools.partial(_sbd_kernel, pages=pages, n_new=n_new),
        out_shape=jax.ShapeDtypeStruct((n_seq, n_new, n), F32),
        grid_spec=grid_spec,
        compiler_params=_params("arbitrary", "arbitrary"),
        name="sb_decode",
    )(page_table, q_rep, bias_col, k_new, v_new, gain_row, usum,
      *([cache_k] * pages), *([cache_v] * pages))


def _pad_lanes(v):
    return jnp.pad(v.astype(F32), (0, LANES - v.shape[0])).reshape(1, LANES)


def _run_group(x, weights, *, tm, past):
    n_seq, l, d = x.shape
    depth = weights["norms"].shape[0]
    nq = DN_HEADS * DN_DIM
    sb = SB_HEADS * SB_DIM
    h = x.reshape(n_seq * l, d)
    ks, vs, ss, cs = [], [], [], []
    kv_rows = (None, None)
    for layer in range(depth):
        norms = weights["norms"][layer]
        w_in = weights["w_in"][layer]
        w_qkv = w_in[:, :3 * nq].astype(BF16)
        w_z = w_in[:, 3 * nq:4 * nq].astype(BF16)
        w_ab = jnp.pad(w_in[:, 4 * nq:4 * nq + 2 * DN_HEADS], ((0, 0), (0, LANES - 2 * DN_HEADS))).astype(BF16)
        w_sb = w_in[:, 4 * nq + 2 * DN_HEADS:].astype(BF16)
        w_o = weights["w_o"][layer].astype(BF16)
        a_log = _pad_lanes(weights["a_log"][layer])
        dt_bias = _pad_lanes(weights["dt_bias"][layer])
        dn_gain = weights["dn_out_norm"][layer].reshape(1, DN_DIM)
        sb_bias = weights["sb_logit_bias"][layer]
        sb_gain = weights["sb_out_norm"][layer]

        h = _ffn(h, norms, weights["ffn1_w_gate_up"][layer].astype(BF16),
                 weights["ffn1_w_down"][layer].astype(BF16), 0, 1, tm)
        qkv, z, ab, q_sb, k16, v16, k_sb, v_sb = _inproj(
            h, norms, w_qkv, w_z, w_ab, w_sb, 1 if past else n_seq, tm, None if past else kv_rows)
        conv_w = weights["conv_w"][layer]
        if past is None:
            kv_rows = (k_sb, v_sb)
            hist8 = jnp.zeros((n_seq, SUBLANES, 3 * nq), F32)
            q_dn, k_dn, v_dn = _conv(qkv, hist8, conv_w, n_seq, tm)
            s0 = jnp.zeros((n_seq, DN_HEADS, DN_DIM, DN_DIM), F32)
            o_dn, s_new = _delta(q_dn, k_dn, v_dn, ab, z, s0, a_log, dt_bias, dn_gain, chunk=DN_CHUNK,
                                 n_chunks=DN_CHUNKS_PER_STEP, valid_rows=DN_CHUNK, one_sequence=True)
            o_sb = _sb_prompt(q_sb, k16, v16, sb_bias, jnp.tile(sb_gain, 2).reshape(1, LANES), n_seq,
                              tq=SB_QUERY_ROWS)
            conv_new = qkv.reshape(n_seq, l, 3 * nq)[:, l - (CONV_W - 1):]
        else:
            conv_state, delta_state, cache_k, cache_v, page_table = past
            qkv3 = qkv.reshape(n_seq, l, 3 * nq)
            hist = conv_state[layer]
            tile = jnp.concatenate(
                [hist, qkv3, jnp.zeros((n_seq, SUBLANES - l - hist.shape[1], 3 * nq), F32)], axis=1)
            tile = tile.reshape(n_seq * SUBLANES, 3 * nq)
            conv_out = _conv(tile, jnp.zeros((1, SUBLANES, 3 * nq), F32), conv_w, 1, n_seq * SUBLANES)
            first = hist.shape[1]

            def one_chunk(a):
                return jnp.pad(a, ((0, 0), (0, SAMPLE_CHUNK - l), (0, 0))).reshape(n_seq * SAMPLE_CHUNK, -1)

            q_dn, k_dn, v_dn = (one_chunk(a.reshape(n_seq, SUBLANES, nq)[:, first:first + l]) for a in conv_out)
            o_dn, s_new = _delta(q_dn, k_dn, v_dn, one_chunk(ab.reshape(n_seq, l, -1)),
                                 one_chunk(z.reshape(n_seq, l, -1)), delta_state[layer], a_log, dt_bias,
                                 dn_gain, chunk=SAMPLE_CHUNK, n_chunks=DN_CHUNKS_PER_STEP, valid_rows=l,
                                 one_sequence=False)
            o_dn = o_dn.reshape(n_seq, SAMPLE_CHUNK, nq)[:, :l].reshape(n_seq * l, nq)
            o_sb = _sb_decode(q_sb.reshape(n_seq, l, sb), k_sb.reshape(n_seq, l, sb), v_sb.reshape(n_seq, l, sb),
                              cache_k, cache_v, page_table, layer, sb_bias, sb_gain, pages=DECODE_PAGES)
            o_sb = o_sb.reshape(n_seq * l, sb)
            conv_new = jnp.concatenate([hist, qkv3], axis=1)[:, -(CONV_W - 1):]
        h = _ffn(h, norms, weights["ffn2_w_gate_up"][layer].astype(BF16),
                 weights["ffn2_w_down"][layer].astype(BF16), 4, 5, tm, mix=(o_dn, o_sb, w_o[:nq], w_o[nq:]))
        if past is not None:
            ks.append(k_sb.reshape(n_seq, l, SB_HEADS, SB_DIM))
            vs.append(v_sb.reshape(n_seq, l, SB_HEADS, SB_DIM))
        ss.append(s_new)
        cs.append(conv_new)
    if past is None:
        k_rows, v_rows = (jnp.transpose(a.reshape(n_seq, depth, SB_HEADS, SB_DIM, l), (0, 4, 1, 2, 3))
                          for a in kv_rows)
    else:
        k_rows, v_rows = jnp.stack(ks, axis=2), jnp.stack(vs, axis=2)
    return h.reshape(n_seq, l, d), k_rows, v_rows, jnp.stack(ss, axis=0), jnp.stack(cs, axis=0)


def kernel(x_prompt, x_sample, cache_sb_k, cache_sb_v, page_table, state_delta, state_conv, norms,
           ffn1_w_gate_up, ffn1_w_down, w_in, conv_w, a_log, dt_bias, dn_out_norm, sb_logit_bias, sb_out_norm,
           w_o, ffn2_w_gate_up, ffn2_w_down):
    weights = dict(norms=norms, ffn1_w_gate_up=ffn1_w_gate_up, ffn1_w_down=ffn1_w_down, w_in=w_in,
                   conv_w=conv_w, a_log=a_log, dt_bias=dt_bias, dn_out_norm=dn_out_norm,
                   sb_logit_bias=sb_logit_bias, sb_out_norm=sb_out_norm, w_o=w_o,
                   ffn2_w_gate_up=ffn2_w_gate_up, ffn2_w_down=ffn2_w_down)
    n_phys, page, depth = cache_sb_k.shape[:3]
    cache_k = jnp.transpose(cache_sb_k, (0, 2, 3, 4, 1)).reshape(n_phys, depth, -1, page)
    cache_v = jnp.transpose(cache_sb_v, (0, 2, 3, 4, 1)).reshape(n_phys, depth, -1, page)
    y_p, k_p, v_p, d_p, c_p = _run_group(x_prompt, weights, tm=ROW_TILE, past=None)
    n_tok = x_sample.shape[0] * x_sample.shape[1]
    y_s, k_s, v_s, d_s, c_s = _run_group(x_sample, weights, tm=n_tok,
                                         past=(state_conv, state_delta, cache_k, cache_v, page_table))
    return (y_p, y_s, k_p, v_p, k_s, v_s, d_p, d_s, c_p, c_s)
```

```python
import functools
import math

import jax
import jax.numpy as jnp
from jax import lax
from jax.experimental import pallas as pl
from jax.experimental.pallas import tpu as pltpu

F32 = jnp.float32
BF16 = jnp.bfloat16
HIGHEST = lax.Precision.HIGHEST

EPS = 1e-6
DN_HEADS = 4
DN_DIM = 128
SB_HEADS = 8
SB_DIM = 64
CONV_W = 4
LANES = 128
SUBLANES = 8
VMEM_LIMIT = 56 * 1024 * 1024
MXU_DIM = 256
SUFFIX_SUM_WIDTH = MXU_DIM
SB_QUERY_ROWS = 1024
SB_ITEM_ROWS = 512
DECODE_PAGES = 32
ROW_TILE = 512
FFN_CHUNK = 256
DN_CHUNK = 64
DN_CHUNKS_PER_STEP = 8
LOG2E = 1.4426950408889634
LOCAL_GROUP = 8
SAMPLE_CHUNK = 16


def _dot(a, b, precision=None):
    return jnp.dot(a, b, preferred_element_type=F32, precision=precision)


def _dot_nt(a, b):
    return lax.dot_general(a, b, (((1,), (1,)), ((), ())), preferred_element_type=F32)


def _dot_tn(a, b):
    return lax.dot_general(a, b, (((0,), (0,)), ((), ())), preferred_element_type=F32)


def _rms(x, gain):
    ms = jnp.mean(x * x, axis=-1, keepdims=True)
    return x * lax.rsqrt(ms + EPS) * gain


def _sigmoid(x):
    return 1.0 / (1.0 + jnp.exp2(x * -LOG2E))


def _softplus(x):
    return jnp.maximum(x, 0.0) + jnp.log(1.0 + jnp.exp(-jnp.abs(x)))


def _run_staged(stagers):
    results = [None] * len(stagers)
    live = list(enumerate(stagers))
    while live:
        unfinished = []
        for index, stager in live:
            try:
                next(stager)
                unfinished.append((index, stager))
            except StopIteration as stop:
                results[index] = stop.value
        live = unfinished
    return results


def _params(*sem):
    return pltpu.CompilerParams(dimension_semantics=sem, vmem_limit_bytes=VMEM_LIMIT)


def _ffn_kernel(*refs, pre, post, d_ff, chunk, mixed):
    if mixed:
        a_ref, b_ref, wa_ref, wb_ref, x_ref, g_ref, wgu_ref, wd_ref, o_ref, acc_ref = refs
        mix = _dot(a_ref[...].astype(BF16), wa_ref[...]) + _dot(b_ref[...].astype(BF16), wb_ref[...])
        x = x_ref[...] + _rms(mix, g_ref[pre - 1:pre, :])
    else:
        x_ref, g_ref, wgu_ref, wd_ref, o_ref, acc_ref = refs
        x = x_ref[...]
    xn = _rms(x, g_ref[pre:pre + 1, :]).astype(BF16)
    for c in range(d_ff // chunk):
        gate = _dot(xn, wgu_ref[:, c * chunk:(c + 1) * chunk])
        up = _dot(xn, wgu_ref[:, d_ff + c * chunk:d_ff + (c + 1) * chunk])
        h = (gate * _sigmoid(gate) * up).astype(BF16)
        part = _dot(h, wd_ref[c * chunk:(c + 1) * chunk, :])
        if c == 0:
            acc_ref[...] = part
        else:
            acc_ref[...] += part
    o_ref[...] = x + 0.5 * _rms(acc_ref[...], g_ref[post:post + 1, :])


def _ffn(x, norms, w_gu, w_d, pre, post, tm, mix=None):
    t, d = x.shape
    d_ff = w_d.shape[0]
    const = lambda i: (0, 0)
    row = lambda n: pl.BlockSpec((tm, n), lambda i: (i, 0))
    whole = lambda w: pl.BlockSpec(w.shape, const, pipeline_mode=pl.Buffered(1))
    extra = ()
    extra_specs = []
    if mix is not None:
        a, b, w_a, w_b = mix
        extra = (a, b, w_a, w_b)
        extra_specs = [row(a.shape[1]), row(b.shape[1]), whole(w_a), whole(w_b)]
    return pl.pallas_call(
        functools.partial(_ffn_kernel, pre=pre, post=post, d_ff=d_ff, chunk=FFN_CHUNK, mixed=mix is not None),
        out_shape=jax.ShapeDtypeStruct((t, d), F32),
        grid=(t // tm,),
        in_specs=extra_specs + [row(d), whole(norms), whole(w_gu), whole(w_d)],
        out_specs=row(d),
        scratch_shapes=[pltpu.VMEM((tm, d), F32)],
        compiler_params=_params("arbitrary"),
        name="ffn",
    )(*extra, x, norms, w_gu, w_d)


def _inproj_kernel(*refs, sb, n_prev, feature_major):
    h_ref, g_ref, wqkv_ref, wz_ref, wab_ref, wsb_ref = refs[:6]
    prev = refs[6:8] if n_prev else ()
    qkv_ref, z_ref, ab_ref, q_ref, kb_ref, vb_ref, k_ref, v_ref = refs[6 + len(prev):]
    u = _rms(h_ref[...], g_ref[2:3, :]).astype(BF16)
    qkv_ref[...] = _dot(u, wqkv_ref[...])
    z_ref[...] = _dot(u, wz_ref[...])
    ab_ref[...] = _dot(u, wab_ref[...])
    p = _dot(u, wsb_ref[...])
    k = p[:, sb:2 * sb]
    v = p[:, 2 * sb:]
    q_ref[...] = p[:, :sb]
    kb_ref[...] = k.astype(BF16)
    vb_ref[...] = v.astype(BF16)
    if feature_major:
        for new, old in zip((k_ref, v_ref), prev):
            new[0, 0:n_prev] = old[0]
        k_ref[0, n_prev] = k.T
        v_ref[0, n_prev] = v.T
    else:
        k_ref[...] = k
        v_ref[...] = v


def _inproj(h, norms, w_qkv, w_z, w_ab, w_sb, n_seq, tm, prev_rows):
    t, d = h.shape
    sb = SB_HEADS * SB_DIM
    nt = t // n_seq // tm
    const = lambda b, i: (0, 0)
    row = lambda n: pl.BlockSpec((tm, n), lambda b, i: (b * nt + i, 0))
    widths = (w_qkv.shape[1], w_z.shape[1], LANES, sb, sb, sb)
    dtypes = (F32, F32, F32, F32, BF16, BF16)
    out_shape = [jax.ShapeDtypeStruct((t, n), dt) for n, dt in zip(widths, dtypes)]
    out_specs = [row(n) for n in widths]
    prev = ()
    n_prev = 0
    if prev_rows is None:
        out_shape += [jax.ShapeDtypeStruct((t, sb), F32)] * 2
        out_specs += [row(sb)] * 2
    else:
        if prev_rows[0] is not None:
            prev = tuple(prev_rows)
            n_prev = prev[0].shape[1]
        layers = lambda n: pl.BlockSpec((1, n, sb, tm), lambda b, i: (b, 0, 0, i))
        out_shape += [jax.ShapeDtypeStruct((n_seq, n_prev + 1, sb, t // n_seq), F32)] * 2
        out_specs += [layers(n_prev + 1)] * 2
    return pl.pallas_call(
        functools.partial(_inproj_kernel, sb=sb, n_prev=n_prev, feature_major=prev_rows is not None),
        out_shape=out_shape,
        grid=(n_seq, nt),
        in_specs=[row(d), pl.BlockSpec(norms.shape, const)]
                 + [pl.BlockSpec(w.shape, const) for w in (w_qkv, w_z, w_ab, w_sb)]
                 + [layers(n_prev) for _ in prev],
        out_specs=out_specs,
        compiler_params=_params("arbitrary", "arbitrary"),
        name="inproj",
    )(h, norms, w_qkv, w_z, w_ab, w_sb, *prev)


def _conv_kernel(x_ref, prev_ref, hist_ref, w_ref, q_ref, k_ref, v_ref):
    x = x_ref[...]
    w = w_ref[...]
    prev = jnp.where(pl.program_id(1) == 0, hist_ref[0], prev_ref[...])
    row = lax.broadcasted_iota(jnp.int32, prev.shape, 0)
    y = x * w[CONV_W - 1:CONV_W, :]
    y_head = y[0:SUBLANES]
    for s in range(1, CONV_W):
        tap = w[CONV_W - 1 - s:CONV_W - s, :]
        xs = pltpu.roll(x, s, 0)
        y = y + xs * tap
        head = jnp.where(row < s, pltpu.roll(prev, s, 0), xs[0:SUBLANES])
        y_head = y_head + head * tap

    def finish(y, rows):
        y = y * _sigmoid(y)
        nq = DN_HEADS * DN_DIM
        for h in range(DN_HEADS):
            for ref, off, scale in ((q_ref, 0, DN_DIM ** -0.5), (k_ref, nq, 1.0)):
                t = y[:, off + h * DN_DIM:off + (h + 1) * DN_DIM]
                ss = jnp.sum(t * t, axis=-1, keepdims=True)
                ref[rows, h * DN_DIM:(h + 1) * DN_DIM] = t * (lax.rsqrt(ss + EPS) * scale)
        v_ref[rows, :] = y[:, 2 * nq:]

    finish(y, slice(None))
    finish(y_head, slice(0, SUBLANES))


def _conv(x, hist8, w, n_seq, tm):
    t, c = x.shape
    nt = t // n_seq // tm
    blocks8 = tm // SUBLANES
    n = DN_HEADS * DN_DIM
    out = pl.BlockSpec((tm, n), lambda b, i: (b * nt + i, 0))
    return pl.pallas_call(
        _conv_kernel,
        out_shape=[jax.ShapeDtypeStruct((t, n), F32)] * 3,
        grid=(n_seq, nt),
        in_specs=[pl.BlockSpec((tm, c), lambda b, i: (b * nt + i, 0)),
                  pl.BlockSpec((SUBLANES, c), lambda b, i: (jnp.maximum((b * nt + i) * blocks8 - 1, 0), 0)),
                  pl.BlockSpec((1, SUBLANES, c), lambda b, i: (b, 0, 0)),
                  pl.BlockSpec(w.shape, lambda b, i: (0, 0))],
        out_specs=[out, out, out],
        compiler_params=_params("arbitrary", "arbitrary"),
        name="conv",
    )(x, x, hist8, w)


def _split(x, terms):
    parts = []
    for _ in range(terms):
        p = x.astype(BF16)
        parts.append(p)
        x = x - p.astype(F32)
    return parts


def _dot_left01(a01, b):
    return sum(_dot(a01, p) for p in _split(b, 3))


def _dot_split(lefts, b):
    b_hi, b_lo = _split(b, 2)
    out = []
    for a in lefts:
        a_hi, a_lo = _split(a, 2)
        out.append(_dot(a_hi, b_hi) + _dot(a_lo, b_hi) + _dot(a_hi, b_lo))
    return out


def _delta_kernel(q_ref, k_ref, v_ref, ab_ref, z_ref, s0_ref, alog_ref, dtb_ref, gain_ref,
                  o_ref, s_ref, u_ref, w_ref, qk_ref, qe_ref, kd_ref, decay_last_ref,
                  *, chunk, n_chunks, valid_rows, one_sequence):
    c = chunk

    @pl.when(pl.program_id(1) == 0)
    def _():
        s_ref[...] = s0_ref[...]

    r = DN_HEADS * c
    ri = lax.broadcasted_iota(jnp.int32, (r, r), 0)
    ci = lax.broadcasted_iota(jnp.int32, (r, r), 1)
    same_head = (ri // c) == (ci // c)
    incl = same_head & (ri >= ci)
    strict = same_head & (ri > ci)
    eye = ri == ci
    ones = jnp.ones((SUBLANES, r), BF16)
    ltri = (lax.broadcasted_iota(jnp.int32, (c, c), 0) >= lax.broadcasted_iota(jnp.int32, (c, c), 1)).astype(BF16)
    neg_a = -jnp.exp(alog_ref[...])
    dtb = dtb_ref[...]
    gain = gain_ref[...]
    n_doublings = max(c.bit_length() - 1, 0)
    heads = range(DN_HEADS)

    def head_cols(h):
        return slice(h * DN_DIM, (h + 1) * DN_DIM)

    def stacked_rows(ch):
        return pl.ds(pl.multiple_of(ch * r, r), r)

    def local(ch):
        rows = pl.ds(pl.multiple_of(ch * c, c), c)
        gb = ab_ref[rows, :]
        g_all = neg_a * _softplus(gb + dtb)
        beta_all = _sigmoid(gb)
        if valid_rows < c:
            live = lax.broadcasted_iota(jnp.int32, g_all.shape, 0) < valid_rows
            g_all = jnp.where(live, g_all, 0.0)
            beta_all = jnp.where(live, beta_all, 0.0)
        gcum_all = _dot_left01(ltri, g_all)
        yield
        decay_last_ref[pl.ds(pl.multiple_of(ch * SUBLANES, SUBLANES), SUBLANES), :] = jnp.broadcast_to(
            jnp.exp(gcum_all[c - 1:c, :]), (SUBLANES, LANES))
        beta = jnp.concatenate([beta_all[:, DN_HEADS + h:DN_HEADS + h + 1] for h in heads], axis=0)
        gcum = jnp.concatenate([gcum_all[:, h:h + 1] for h in heads], axis=0)
        g_last = jnp.concatenate([jnp.broadcast_to(gcum_all[c - 1:c, h:h + 1], (c, 1)) for h in heads], axis=0)
        gcum_cols = jnp.broadcast_to(gcum, (r, r))
        gcum_row = _dot_left01(ones, jnp.where(eye, gcum_cols, 0.0))[0:1, :]
        yield
        decay = jnp.where(incl, jnp.exp(jnp.minimum(gcum_cols - gcum_row, 0.0)), 0.0)
        q = jnp.concatenate([q_ref[rows, head_cols(h)] for h in heads], axis=0)
        k = jnp.concatenate([k_ref[rows, head_cols(h)] for h in heads], axis=0)
        v = jnp.concatenate([v_ref[rows, head_cols(h)] for h in heads], axis=0)
        kb = k * beta
        with_k = _dot_nt(jnp.concatenate([kb, q], axis=0).astype(BF16), k.astype(BF16))
        yield
        m = jnp.where(strict, with_k[:r] * decay, 0.0)
        out = stacked_rows(ch)
        qk_ref[out, :] = (with_k[r:] * decay).astype(BF16)
        inv = jnp.where(eye, 1.0, 0.0) - m
        pw, = _dot_split([m], m)
        yield
        for step in range(1, n_doublings):
            if step + 1 < n_doublings:
                grown, pw = _dot_split([inv, pw], pw)
            else:
                grown, = _dot_split([inv], pw)
            inv = inv + grown
            yield
        e_gcum = jnp.exp(gcum)
        sol, = _dot_split([inv], jnp.concatenate([v * beta, kb * e_gcum], axis=1))
        u_ref[out, :] = sol[:, :DN_DIM]
        w_ref[out, :] = sol[:, DN_DIM:].astype(BF16)
        qe_ref[out, :] = (q * e_gcum).astype(BF16)
        kd_ref[out, :] = (k * jnp.exp(g_last - gcum)).astype(BF16)

    group = math.gcd(n_chunks, LOCAL_GROUP)

    def local_group(i, carry):
        _run_staged([local(group * i + g) for g in range(group)])
        return carry

    lax.fori_loop(0, n_chunks // group, local_group, 0)

    def scan(ch):
        rows = pl.ds(pl.multiple_of(ch * c, c), c)
        decay_last = decay_last_ref[pl.ds(pl.multiple_of(ch * SUBLANES, SUBLANES), SUBLANES), :]
        seq = 0 if one_sequence else ch
        s = [s_ref[seq, h] for h in heads]
        s16 = [x.astype(BF16) for x in s]
        head_rows = [pl.ds(pl.multiple_of(ch * r + h * c, c), c) for h in heads]
        u16 = [(u_ref[head_rows[h], :] - _dot(w_ref[head_rows[h], :], s16[h])).astype(BF16) for h in heads]
        yield
        u16_all = jnp.concatenate(u16, axis=0)
        for h in heads:
            o = _dot(qe_ref[head_rows[h], :], s16[h]) + _dot(qk_ref[head_rows[h], :], u16_all)
            s_ref[seq, h] = s[h] * decay_last[0:1, h:h + 1] + _dot_tn(kd_ref[head_rows[h], :], u16[h])
            zz = z_ref[rows, head_cols(h)]
            o_ref[rows, head_cols(h)] = _rms(o, gain) * (zz * _sigmoid(zz))

    if one_sequence:
        def scan_step(ch, carry):
            _run_staged([scan(ch)])
            return carry
        lax.fori_loop(0, n_chunks, scan_step, 0)
    else:
        _run_staged([scan(ch) for ch in range(n_chunks)])


def _delta(q, k, v, ab, z, s0, a_log, dt_bias, gain, chunk, n_chunks, valid_rows, one_sequence):
    t, n = q.shape
    rows = chunk * n_chunks
    stacked = DN_HEADS * rows
    n_seq = s0.shape[0]
    outer = n_seq if one_sequence else n_seq // n_chunks
    steps = t // outer // rows
    blk = lambda w: pl.BlockSpec((rows, w), lambda b, i: (b * steps + i, 0))
    state = pl.BlockSpec((n_seq // outer,) + s0.shape[1:], lambda b, i: (b, 0, 0, 0))
    vec = pl.BlockSpec((1, LANES), lambda b, i: (0, 0))
    return pl.pallas_call(
        functools.partial(_delta_kernel, chunk=chunk, n_chunks=n_chunks, valid_rows=valid_rows,
                          one_sequence=one_sequence),
        out_shape=[jax.ShapeDtypeStruct((t, n), F32), jax.ShapeDtypeStruct(s0.shape, F32)],
        grid=(outer, steps),
        in_specs=[blk(n), blk(n), blk(n), blk(LANES), blk(n), state, vec, vec, vec],
        out_specs=[blk(n), state],
        scratch_shapes=[pltpu.VMEM((stacked, DN_DIM), F32), pltpu.VMEM((stacked, DN_DIM), BF16),
                        pltpu.VMEM((stacked, DN_HEADS * chunk), BF16),
                        pltpu.VMEM((stacked, DN_DIM), BF16), pltpu.VMEM((stacked, DN_DIM), BF16),
                        pltpu.VMEM((n_chunks * SUBLANES, LANES), F32)],
        compiler_params=_params("arbitrary", "arbitrary"),
        name="delta",
    )(q, k, v, ab, z, s0, a_log, dt_bias, gain)


def _softplus2(x):
    return jnp.maximum(x, 0.0) + jnp.log(1.0 + jnp.exp2(jnp.minimum(x, -x))) * LOG2E


def _sb_weights(z2, later, usum, valid=None):
    n = usum.shape[0]
    sp = _softplus2(z2)
    if valid is not None:
        sp = jnp.where(valid, sp, 0.0)
    pieces = []
    for b in reversed(range(z2.shape[1] // n)):
        cols = slice(b * n, (b + 1) * n)
        within = _dot(sp[:, cols].astype(BF16), usum)
        pieces.append(jnp.exp2(z2[:, cols] - within - later))
        later = later + jnp.sum(sp[:, cols], axis=-1, keepdims=True)
    a = pieces[0] if len(pieces) == 1 else jnp.concatenate(pieces[::-1], axis=1)
    if valid is not None:
        a = jnp.where(valid, a, 0.0)
    return a.astype(BF16), later


def _suffix_sum_matrix(n):
    j = lax.broadcasted_iota(jnp.int32, (n, n), 0)
    s = lax.broadcasted_iota(jnp.int32, (n, n), 1)
    return (j >= s).astype(BF16)


def _sbp_kernel(bias_ref, q_ref, k_ref, v_ref, gain_ref, usum_ref, bd_ref, o_ref, *, tq, sub):
    pair = pl.program_id(1)
    i = pl.program_id(2)
    n_sub = tq // sub
    items = [(hh, s) for hh in range(2) for s in range(n_sub)]
    first = lax.broadcasted_iota(jnp.int32, (sub, LANES), 1) < SB_DIM

    def queries(hh, s):
        q = q_ref[0, s * sub:(s + 1) * sub, :] * (SB_DIM ** -0.5 * LOG2E)
        return (jnp.where(first, q, 0.0) if hh == 0 else jnp.where(first, 0.0, q)).astype(BF16)

    q_items = [queries(hh, s) for hh, s in items]
    bias = (bias_ref[2 * pair] * LOG2E, bias_ref[2 * pair + 1] * LOG2E)
    usum = usum_ref[...]

    def block(j, carry, diagonal):
        start = pl.multiple_of(j * tq, tq)

        def keys_of(ref, s):
            return ref[0, pl.ds(start, (s + 1) * sub if diagonal else tq), :]

        def causal(s):
            key = lax.broadcasted_iota(jnp.int32, (sub, (s + 1) * sub), 1)
            return key < lax.broadcasted_iota(jnp.int32, (sub, (s + 1) * sub), 0) + s * sub

        state = list(carry)
        logits, weights = {}, {}
        for step in range(len(items) + 2):
            if step < len(items):
                hh, s = items[step]
                logits[step] = _dot_nt(q_items[step], keys_of(k_ref, s)) + bias[hh]
            if 0 <= step - 1 < len(items):
                n = step - 1
                later, acc = state[n]
                weights[n], later = _sb_weights(logits.pop(n), later, usum,
                                                causal(items[n][1]) if diagonal else None)
                state[n] = (later, acc)
            if 0 <= step - 2 < len(items):
                n = step - 2
                later, acc = state[n]
                state[n] = (later, acc + _dot(weights.pop(n), keys_of(v_ref, items[n][1])))
        return tuple(state)

    zero = (jnp.zeros((sub, 1), F32), jnp.zeros((sub, LANES), F32))
    carry = block(i, (zero,) * len(items), True)
    carry = lax.fori_loop(0, i, lambda t, c: block(i - 1 - t, c, False), carry)
    for s in range(n_sub):
        o = jnp.where(first, carry[s][1], carry[n_sub + s][1])
        ss = _dot(o * o, bd_ref[...], HIGHEST)
        o_ref[0, s * sub:(s + 1) * sub, :] = o * lax.rsqrt(ss * (1.0 / SB_DIM) + EPS) * gain_ref[...]


def _sb_prompt(q, k16, v16, bias, gain2, n_seq, tq):
    t, n = q.shape
    l = t // n_seq
    pairs = n // LANES
    q3 = q.reshape(n_seq, l, n)
    k3 = k16.reshape(n_seq, l, n)
    v3 = v16.reshape(n_seq, l, n)
    seg = jnp.arange(LANES) // SB_DIM
    bd = (seg[:, None] == seg[None, :]).astype(F32)
    usum = _suffix_sum_matrix(SUFFIX_SUM_WIDTH)
    kv = pl.BlockSpec((1, l, LANES), lambda b, p, i: (b, 0, p))
    const = lambda b, p, i: (0, 0)
    out = pl.pallas_call(
        functools.partial(_sbp_kernel, tq=tq, sub=SB_ITEM_ROWS),
        out_shape=jax.ShapeDtypeStruct((n_seq, l, n), F32),
        grid=(n_seq, pairs, l // tq),
        in_specs=[pl.BlockSpec(memory_space=pltpu.SMEM),
                  pl.BlockSpec((1, tq, LANES), lambda b, p, i: (b, i, p)),
                  kv, kv,
                  pl.BlockSpec((1, LANES), const),
                  pl.BlockSpec(usum.shape, const),
                  pl.BlockSpec(bd.shape, const)],
        out_specs=pl.BlockSpec((1, tq, LANES), lambda b, p, i: (b, i, p)),
        compiler_params=_params("arbitrary", "arbitrary", "arbitrary"),
        name="sb_prompt",
    )(bias, q3, k3, v3, gain2, usum, bd)
    return out.reshape(t, n)


def _sbd_kernel(pt_ref, q_ref, bias_ref, knew_ref, vnew_ref, gain_ref, usum_ref, *rest, pages, n_new):
    k_pages = rest[:pages]
    v_pages = rest[pages:2 * pages]
    o_ref = rest[2 * pages]
    later_ref, acc_ref, kpad_ref, vpad_ref = rest[2 * pages + 1:]
    step = pl.program_id(1)
    n_rows, width = acc_ref.shape
    page = kpad_ref.shape[0]
    own_head = (lax.broadcasted_iota(jnp.int32, (n_rows, width), 1) // SB_DIM
                == lax.broadcasted_iota(jnp.int32, (n_rows, width), 0) % SB_HEADS)
    q = jnp.where(own_head, q_ref[0] * (SB_DIM ** -0.5 * LOG2E), 0.0).astype(BF16)
    bias = bias_ref[...] * LOG2E
    usum = usum_ref[...]

    def block(z2, weighted_sum, valid):
        a, later = _sb_weights(z2 + bias, later_ref[...], usum, valid)
        later_ref[...] = later
        acc_ref[...] += weighted_sum(a)

    @pl.when(step == 0)
    def _():
        later_ref[...] = jnp.zeros_like(later_ref)
        acc_ref[...] = jnp.zeros_like(acc_ref)
        kpad_ref[...] = jnp.zeros_like(kpad_ref)
        vpad_ref[...] = jnp.zeros_like(vpad_ref)
        kpad_ref[0:n_new, :] = knew_ref[0]
        vpad_ref[0:n_new, :] = vnew_ref[0]
        key = lax.broadcasted_iota(jnp.int32, (n_rows, page), 1)
        tok = lax.broadcasted_iota(jnp.int32, (n_rows, page), 0) // SB_HEADS
        vs = vpad_ref[...].astype(BF16)
        block(_dot_nt(q, kpad_ref[...].astype(BF16)), lambda a: _dot(a, vs), key < tok)

    def weighted_sum(a):
        return sum(_dot_nt(a[:, i * page:(i + 1) * page], v_pages[i][0, 0].astype(BF16)) for i in range(pages))

    block(jnp.concatenate([_dot(q, k_pages[i][0, 0].astype(BF16)) for i in range(pages)], axis=1),
          weighted_sum, None)

    @pl.when(step == pl.num_programs(1) - 1)
    def _():
        o = jnp.where(own_head, acc_ref[...], 0.0)
        ss = jnp.sum(o * o, axis=-1, keepdims=True)
        o = o * lax.rsqrt(ss * (1.0 / SB_DIM) + EPS)
        o = jnp.sum(o.reshape(n_rows // SB_HEADS, SB_HEADS, width), axis=1)
        o_ref[0] = o * gain_ref[...]


def _sb_decode(q, k_new, v_new, cache_k, cache_v, page_table, layer, bias, gain, pages):
    n_seq, n_new, n = q.shape
    page = cache_k.shape[3]
    n_pages = page_table.shape[1]
    steps = n_pages // pages
    n_rows = n_new * SB_HEADS
    q_rep = jnp.repeat(q, SB_HEADS, axis=1)
    bias_col = jnp.tile(bias, n_new).reshape(n_rows, 1)
    gain_row = jnp.tile(gain, SB_HEADS).reshape(1, n)
    usum = _suffix_sum_matrix(page)

    def page_spec(i):
        return pl.BlockSpec((1, 1, n, page),
                            lambda b, s, pt: (pt[b, (steps - 1 - s) * pages + i], layer, 0, 0))

    per_seq = lambda r: pl.BlockSpec((1, r, n), lambda b, s, pt: (b, 0, 0))
    const = lambda b, s, pt: (0, 0)
    grid_spec = pltpu.PrefetchScalarGridSpec(
        num_scalar_prefetch=1,
        grid=(n_seq, steps),
        in_specs=[per_seq(n_rows),
                  pl.BlockSpec(bias_col.shape, const),
                  per_seq(n_new), per_seq(n_new),
                  pl.BlockSpec(gain_row.shape, const),
                  pl.BlockSpec(usum.shape, const)]
                 + [page_spec(i) for i in range(pages)] * 2,
        out_specs=per_seq(n_new),
        scratch_shapes=[pltpu.VMEM((n_rows, 1), F32), pltpu.VMEM((n_rows, n), F32),
                        pltpu.VMEM((page, n), F32), pltpu.VMEM((page, n), F32)],
    )
    return pl.pallas_call(
        functools.partial(_sbd_kernel, pages=pages, n_new=n_new),
        out_shape=jax.ShapeDtypeStruct((n_seq, n_new, n), F32),
        grid_spec=grid_spec,
        compiler_params=_params("arbitrary", "arbitrary"),
        name="sb_decode",
    )(page_table, q_rep, bias_col, k_new, v_new, gain_row, usum,
      *([cache_k] * pages), *([cache_v] * pages))


def _pad_lanes(v):
    return jnp.pad(v.astype(F32), (0, LANES - v.shape[0])).reshape(1, LANES)


def _run_group(x, weights, *, tm, past):
    n_seq, l, d = x.shape
    depth = weights["norms"].shape[0]
    nq = DN_HEADS * DN_DIM
    sb = SB_HEADS * SB_DIM
    h = x.reshape(n_seq * l, d)
    ks, vs, ss, cs = [], [], [], []
    kv_rows = (None, None)
    for layer in range(depth):
        norms = weights["norms"][layer]
        w_in = weights["w_in"][layer]
        w_qkv = w_in[:, :3 * nq].astype(BF16)
        w_z = w_in[:, 3 * nq:4 * nq].astype(BF16)
        w_ab = jnp.pad(w_in[:, 4 * nq:4 * nq + 2 * DN_HEADS], ((0, 0), (0, LANES - 2 * DN_HEADS))).astype(BF16)
        w_sb = w_in[:, 4 * nq + 2 * DN_HEADS:].astype(BF16)
        w_o = weights["w_o"][layer].astype(BF16)
        a_log = _pad_lanes(weights["a_log"][layer])
        dt_bias = _pad_lanes(weights["dt_bias"][layer])
        dn_gain = weights["dn_out_norm"][layer].reshape(1, DN_DIM)
        sb_bias = weights["sb_logit_bias"][layer]
        sb_gain = weights["sb_out_norm"][layer]

        h = _ffn(h, norms, weights["ffn1_w_gate_up"][layer].astype(BF16),
                 weights["ffn1_w_down"][layer].astype(BF16), 0, 1, tm)
        qkv, z, ab, q_sb, k16, v16, k_sb, v_sb = _inproj(
            h, norms, w_qkv, w_z, w_ab, w_sb, 1 if past else n_seq, tm, None if past else kv_rows)
        conv_w = weights["conv_w"][layer]
        if past is None:
            kv_rows = (k_sb, v_sb)
            hist8 = jnp.zeros((n_seq, SUBLANES, 3 * nq), F32)
            q_dn, k_dn, v_dn = _conv(qkv, hist8, conv_w, n_seq, tm)
            s0 = jnp.zeros((n_seq, DN_HEADS, DN_DIM, DN_DIM), F32)
            o_dn, s_new = _delta(q_dn, k_dn, v_dn, ab, z, s0, a_log, dt_bias, dn_gain, chunk=DN_CHUNK,
                                 n_chunks=DN_CHUNKS_PER_STEP, valid_rows=DN_CHUNK, one_sequence=True)
            o_sb = _sb_prompt(q_sb, k16, v16, sb_bias, jnp.tile(sb_gain, 2).reshape(1, LANES), n_seq,
                              tq=SB_QUERY_ROWS)
            conv_new = qkv.reshape(n_seq, l, 3 * nq)[:, l - (CONV_W - 1):]
        else:
            conv_state, delta_state, cache_k, cache_v, page_table = past
            qkv3 = qkv.reshape(n_seq, l, 3 * nq)
            hist = conv_state[layer]
            tile = jnp.concatenate(
                [hist, qkv3, jnp.zeros((n_seq, SUBLANES - l - hist.shape[1], 3 * nq), F32)], axis=1)
            tile = tile.reshape(n_seq * SUBLANES, 3 * nq)
            conv_out = _conv(tile, jnp.zeros((1, SUBLANES, 3 * nq), F32), conv_w, 1, n_seq * SUBLANES)
            first = hist.shape[1]

            def one_chunk(a):
                return jnp.pad(a, ((0, 0), (0, SAMPLE_CHUNK - l), (0, 0))).reshape(n_seq * SAMPLE_CHUNK, -1)

            q_dn, k_dn, v_dn = (one_chunk(a.reshape(n_seq, SUBLANES, nq)[:, first:first + l]) for a in conv_out)
            o_dn, s_new = _delta(q_dn, k_dn, v_dn, one_chunk(ab.reshape(n_seq, l, -1)),
                                 one_chunk(z.reshape(n_seq, l, -1)), delta_state[layer], a_log, dt_bias,
                                 dn_gain, chunk=SAMPLE_CHUNK, n_chunks=DN_CHUNKS_PER_STEP, valid_rows=l,
                                 one_sequence=False)
            o_dn = o_dn.reshape(n_seq, SAMPLE_CHUNK, nq)[:, :l].reshape(n_seq * l, nq)
            o_sb = _sb_decode(q_sb.reshape(n_seq, l, sb), k_sb.reshape(n_seq, l, sb), v_sb.reshape(n_seq, l, sb),
                              cache_k, cache_v, page_table, layer, sb_bias, sb_gain, pages=DECODE_PAGES)
            o_sb = o_sb.reshape(n_seq * l, sb)
            conv_new = jnp.concatenate([hist, qkv3], axis=1)[:, -(CONV_W - 1):]
        h = _ffn(h, norms, weights["ffn2_w_gate_up"][layer].astype(BF16),
                 weights["ffn2_w_down"][layer].astype(BF16), 4, 5, tm, mix=(o_dn, o_sb, w_o[:nq], w_o[nq:]))
        if past is not None:
            ks.append(k_sb.reshape(n_seq, l, SB_HEADS, SB_DIM))
            vs.append(v_sb.reshape(n_seq, l, SB_HEADS, SB_DIM))
        ss.append(s_new)
        cs.append(conv_new)
    if past is None:
        k_rows, v_rows = (jnp.transpose(a.reshape(n_seq, depth, SB_HEADS, SB_DIM, l), (0, 4, 1, 2, 3))
                          for a in kv_rows)
    else:
        k_rows, v_rows = jnp.stack(ks, axis=2), jnp.stack(vs, axis=2)
    return h.reshape(n_seq, l, d), k_rows, v_rows, jnp.stack(ss, axis=0), jnp.stack(cs, axis=0)


def kernel(x_prompt, x_sample, cache_sb_k, cache_sb_v, page_table, state_delta, state_conv, norms,
           ffn1_w_gate_up, ffn1_w_down, w_in, conv_w, a_log, dt_bias, dn_out_norm, sb_logit_bias, sb_out_norm,
           w_o, ffn2_w_gate_up, ffn2_w_down):
    weights = dict(norms=norms, ffn1_w_gate_up=ffn1_w_gate_up, ffn1_w_down=ffn1_w_down, w_in=w_in,
                   conv_w=conv_w, a_log=a_log, dt_bias=dt_bias, dn_out_norm=dn_out_norm,
                   sb_logit_bias=sb_logit_bias, sb_out_norm=sb_out_norm, w_o=w_o,
                   ffn2_w_gate_up=ffn2_w_gate_up, ffn2_w_down=ffn2_w_down)
    n_phys, page, depth = cache_sb_k.shape[:3]
    cache_k = jnp.transpose(cache_sb_k, (0, 2, 3, 4, 1)).reshape(n_phys, depth, -1, page)
    cache_v = jnp.transpose(cache_sb_v, (0, 2, 3, 4, 1)).reshape(n_phys, depth, -1, page)
    y_p, k_p, v_p, d_p, c_p = _run_group(x_prompt, weights, tm=ROW_TILE, past=None)
    n_tok = x_sample.shape[0] * x_sample.shape[1]
    y_s, k_s, v_s, d_s, c_s = _run_group(x_sample, weights, tm=n_tok,
                                         past=(state_conv, state_delta, cache_k, cache_v, page_table))
    return (y_p, y_s, k_p, v_p, k_s, v_s, d_p, d_s, c_p, c_s)
```
